```python
import math
import jax
import jax.numpy as jnp
from jax import lax
import numpy as np

D_MODEL = 1024
BATCH = 16
SEQ = 2048
DEPTH = 4

N_MIXERS = 4
EXPAND = 2
D_INNER = EXPAND * D_MODEL
EPS = 1e-6

SSD_HEAD_DIM = 64
SSD_HEADS = D_INNER // SSD_HEAD_DIM
SSD_GROUPS = 8
SSD_HPG = SSD_HEADS // SSD_GROUPS
SSD_STATE = 128
SSD_CONV = 5
SSD_CHUNK = 128
SSD_CONV_DIM = D_INNER + 2 * SSD_GROUPS * SSD_STATE
SSD_PROJ = D_INNER + SSD_CONV_DIM + 2 * SSD_HEADS

GLA_HEADS = 4
GLA_DK = D_MODEL // 2 // GLA_HEADS
GLA_DV = D_INNER // GLA_HEADS
GLA_RANK = 16
GLA_NORMALIZER = 16.0
GLA_CHUNK = 64
GLA_QK = GLA_HEADS * GLA_DK
GLA_PROJ = 2 * GLA_QK + 2 * D_INNER + 2 * GLA_RANK

HY_ORDER = 2
HY_SHORT = 3
HY_EMB = 33
HY_BANDS = (HY_EMB - 1) // 2
HY_FFN = 64
HY_INNER = 2
HY_FAST_PCT = 0.3
HY_SLOW_PCT = 1.5
HY_TARGET = 1e-2
HY_PROJ = (HY_ORDER + 2) * D_INNER
HY_FILT = 2 * HY_ORDER * D_INNER

ML_HEADS = 4
ML_DH = D_INNER // ML_HEADS
ML_DK = ML_DH // 2
ML_CONV = 5
ML_CHUNK = 64
ML_PROJ = 3 * D_INNER + 4 * ML_HEADS

kernel_name = "bidir_hybrid_ssd_gla_hyena_mlstm"


def rmsnorm(x, g):
    xf = x.astype(jnp.float32)
    y = xf * lax.rsqrt(jnp.mean(xf * xf, axis=-1, keepdims=True) + EPS)
    return (y * g.astype(jnp.float32)).astype(x.dtype)


def dwconv(x, w, b):
    k = w.shape[0]
    y = lax.conv_general_dilated(x, w[:, None, :].astype(x.dtype), (1,), [((k - 1) // 2, (k - 1) // 2)],
                                 dimension_numbers=('NWC', 'WIO', 'NWC'), feature_group_count=x.shape[-1])
    return y + b.astype(x.dtype)


def rev(t):
    return jnp.flip(t, axis=1)


def to_chunks(t, q):
    b, l = t.shape[0], t.shape[1]
    return jnp.moveaxis(t.reshape((b, l // q, q) + t.shape[2:]), 1, 0)


def from_chunks(t):
    t = jnp.moveaxis(t, 0, 1)
    return t.reshape((t.shape[0], t.shape[1] * t.shape[2]) + t.shape[3:])


def ssd_scan(xh, dt, a, bm, cm):
    bsz, seq = xh.shape[:2]
    grp = (SSD_GROUPS, SSD_HPG)
    xs = to_chunks(xh.reshape((bsz, seq) + grp + (SSD_HEAD_DIM,)), SSD_CHUNK)
    dts = to_chunks(dt.reshape((bsz, seq) + grp), SSD_CHUNK)
    las = to_chunks((dt * a).reshape((bsz, seq) + grp), SSD_CHUNK)
    bs = to_chunks(bm, SSD_CHUNK)
    cs = to_chunks(cm, SSD_CHUNK)
    causal = jnp.tril(jnp.ones((SSD_CHUNK, SSD_CHUNK), bool))[None, :, :, None, None]

    def step(state, inp):
        xc, dtc, lac, bc, cc = inp
        cum = jnp.cumsum(lac, axis=1)
        decay = jnp.exp(jnp.where(causal, cum[:, :, None] - cum[:, None], -jnp.inf))
        xdt = xc * dtc[..., None]
        cb = jnp.einsum('blgn,bsgn->blsg', cc, bc)
        y = jnp.einsum('blsgr,bsgrp->blgrp', cb[..., None] * decay, xdt)
        y = y + jnp.einsum('blgn,bgrpn->blgrp', cc, state) * jnp.exp(cum)[..., None]
        tail = jnp.exp(cum[:, -1:] - cum)
        state = (state * jnp.exp(cum[:, -1])[..., None, None]
                 + jnp.einsum('bsgn,bsgrp->bgrpn', bc, xdt * tail[..., None]))
        return state, y

    state0 = jnp.zeros((bsz,) + grp + (SSD_HEAD_DIM, SSD_STATE), jnp.float32)
    _, ys = lax.scan(step, state0, (xs, dts, las, bs, cs))
    return from_chunks(ys).reshape(xh.shape)


def ssd_mixer(u, w_in, conv_w, conv_b, dt_bias, a_log, d_skip, gnorm, w_out):
    f32 = jnp.float32
    bsz, seq = u.shape[:2]
    z, xbc, dt = jnp.split(u @ w_in, [D_INNER, D_INNER + SSD_CONV_DIM], axis=-1)
    xbc = jax.nn.silu(dwconv(xbc, conv_w, conv_b)).astype(f32)
    xh, bm, cm = jnp.split(xbc, [D_INNER, D_INNER + SSD_GROUPS * SSD_STATE], axis=-1)
    xh = xh.reshape(bsz, seq, SSD_HEADS, SSD_HEAD_DIM)
    bm = bm.reshape(bsz, seq, SSD_GROUPS, SSD_STATE)
    cm = cm.reshape(bsz, seq, SSD_GROUPS, SSD_STATE)
    dt = jax.nn.softplus(dt.astype(f32).reshape(bsz, seq, 2, SSD_HEADS) + dt_bias.astype(f32))
    a = -jnp.exp(a_log.astype(f32))
    y = (ssd_scan(xh, dt[:, :, 0], a[0], bm, cm)
         + rev(ssd_scan(rev(xh), rev(dt[:, :, 1]), a[1], rev(bm), rev(cm)))
         + xh * d_skip.astype(f32)[:, None])
    y = y.reshape(bsz, seq, D_INNER) * jax.nn.silu(z.astype(f32))
    y = rmsnorm(y.reshape(bsz, seq, SSD_GROUPS, D_INNER // SSD_GROUPS), gnorm.reshape(SSD_GROUPS, -1))
    return y.reshape(bsz, seq, D_INNER).astype(u.dtype) @ w_out


def gla_scan(q, k, v, lg):
    bsz = q.shape[0]
    causal = jnp.tril(jnp.ones((GLA_CHUNK, GLA_CHUNK), bool))[None, None]

    def step(s, inp):
        qc, kc, vc, gc = inp
        cum = jnp.cumsum(gc, axis=1)
        qg = qc * jnp.exp(cum)
        kg = kc * jnp.exp(-cum)
        att = jnp.where(causal, jnp.einsum('blhk,bshk->bhls', qg, kg), 0.0)
        o = jnp.einsum('bhls,bshv->blhv', att, vc) + jnp.einsum('blhk,bhkv->blhv', qg, s)
        kd = kc * jnp.exp(cum[:, -1:] - cum)
        s = s * jnp.exp(cum[:, -1])[..., None] + jnp.einsum('bshk,bshv->bhkv', kd, vc)
        return s, o

    s0 = jnp.zeros((bsz, GLA_HEADS, GLA_DK, GLA_DV), jnp.float32)
    _, os_ = lax.scan(step, s0, tuple(to_chunks(t, GLA_CHUNK) for t in (q, k, v, lg)))
    return from_chunks(os_)


def gla_mixer(u, w_in, w_gate, b_gate, onorm, w_out):
    f32 = jnp.float32
    bsz, seq = u.shape[:2]
    q, k, v, z, gl = jnp.split(u @ w_in, [GLA_QK, 2 * GLA_QK, 2 * GLA_QK + D_INNER, 2 * GLA_QK + 2 * D_INNER], axis=-1)
    q = q.astype(f32).reshape(bsz, seq, GLA_HEADS, GLA_DK) * (GLA_DK ** -0.5)
    k = k.astype(f32).reshape(bsz, seq, GLA_HEADS, GLA_DK)
    v = v.astype(f32).reshape(bsz, seq, GLA_HEADS, GLA_DV)
    gl = gl.astype(f32).reshape(bsz, seq, 2, GLA_RANK)
    lg = jax.nn.log_sigmoid(jnp.einsum('bldr,drk->bldk', gl, w_gate.astype(f32)) + b_gate.astype(f32)) / GLA_NORMALIZER
    lg = lg.reshape(bsz, seq, 2, GLA_HEADS, GLA_DK)
    o = gla_scan(q, k, v, lg[:, :, 0]) + rev(gla_scan(rev(q), rev(k), rev(v), rev(lg[:, :, 1])))
    o = rmsnorm(o, onorm).reshape(bsz, seq, D_INNER) * jax.nn.silu(z.astype(f32))
    return o.astype(u.dtype) @ w_out


def hyena_filters(seq_len, w_in, b_in, w_hid, b_hid, freq, w_out):
    f32 = jnp.float32
    t = jnp.linspace(0.0, 1.0, seq_len, dtype=f32)[:, None]
    pos = jnp.arange(seq_len, dtype=f32)[:, None]
    bands = jnp.linspace(1e-4, HY_BANDS - 1, HY_BANDS, dtype=f32)[None]
    ang = (2.0 * math.pi / seq_len) * pos * bands
    feats = jnp.concatenate([t, jnp.cos(ang), -jnp.sin(ang)], axis=-1)
    freq = freq.astype(f32)
    h = jnp.sin(freq[0] * (feats @ w_in.astype(f32) + b_in.astype(f32)))
    for j in range(HY_INNER):
        h = jnp.sin(freq[j + 1] * (h @ w_hid[j].astype(f32) + b_hid[j].astype(f32)))
    h = (h @ w_out.astype(f32)).reshape(seq_len, 2, HY_ORDER, D_INNER)
    max_decay = math.log(HY_TARGET) / HY_FAST_PCT
    min_decay = math.log(HY_TARGET) / HY_SLOW_PCT
    deltas = jnp.abs(jnp.linspace(min_decay, max_decay, D_INNER, dtype=f32))
    return h * jnp.exp(-t[:, :, None, None] * deltas)


def long_conv(y, h_f, h_b):
    seq = y.shape[1]
    k = jnp.concatenate([h_f, jnp.zeros_like(h_f[:1]), jnp.flip(h_b[1:], axis=0)], axis=0)
    yf = jnp.fft.rfft(y, n=2 * seq, axis=1)
    kf = jnp.fft.rfft(k, axis=0)
    return jnp.fft.irfft(yf * kf[None], n=2 * seq, axis=1)[:, :seq]


def hyena_mixer(u, w_in, conv_w, conv_b, ffn_w_in, ffn_b_in, ffn_w_hid, ffn_b_hid, ffn_freq, ffn_w_out, d_bias, w_out):
    f32 = jnp.float32
    seq = u.shape[1]
    proj = u @ w_in
    sig = dwconv(proj[..., :3 * D_INNER], conv_w, conv_b).astype(f32)
    z = proj[..., 3 * D_INNER:]
    v, x1, x2 = jnp.split(sig, 3, axis=-1)
    filt = hyena_filters(seq, ffn_w_in, ffn_b_in, ffn_w_hid, ffn_b_hid, ffn_freq, ffn_w_out)
    d_bias = d_bias.astype(f32)
    y = v
    for o, gate in enumerate((x1, x2)):
        y = gate * (long_conv(y, filt[:, 0, o], filt[:, 1, o]) + y * d_bias[o])
    y = y * jax.nn.silu(z.astype(f32))
    return y.astype(u.dtype) @ w_out


def mlstm_scan(q, k, v, li, lf):
    bsz = q.shape[0]
    causal = jnp.tril(jnp.ones((ML_CHUNK, ML_CHUNK), bool))[None, :, :, None]

    def step(carry, inp):
        c, n, m = carry
        qc, kc, vc, ic, fc = inp
        b = jnp.cumsum(fc, axis=1)
        logd = jnp.where(causal, b[:, :, None] - b[:, None] + ic[:, None], -jnp.inf)
        inter = b + m[:, None]
        m_row = jnp.maximum(jnp.max(logd, axis=2), inter)
        s = jnp.einsum('blhk,bshk->blsh', qc, kc) * jnp.exp(logd - m_row[:, :, None])
        scale = jnp.exp(inter - m_row)
        num = jnp.einsum('blsh,bshv->blhv', s, vc) + jnp.einsum('blhk,bhkv->blhv', qc, c) * scale[..., None]
        den = jnp.sum(s, axis=2) + jnp.einsum('blhk,bhk->blh', qc, n) * scale
        hc = num / jnp.maximum(jnp.abs(den), jnp.exp(-m_row))[..., None]
        tot = b[:, -1]
        lw = tot[:, None] - b + ic
        m_new = jnp.maximum(tot + m, jnp.max(lw, axis=1))
        w = jnp.exp(lw - m_new[:, None])
        dec = jnp.exp(tot + m - m_new)
        c = c * dec[..., None, None] + jnp.einsum('bsh,bshk,bshv->bhkv', w, kc, vc)
        n = n * dec[..., None] + jnp.einsum('bsh,bshk->bhk', w, kc)
        return (c, n, m_new), hc

    carry0 = (jnp.zeros((bsz, ML_HEADS, ML_DK, ML_DH), jnp.float32),
              jnp.zeros((bsz, ML_HEADS, ML_DK), jnp.float32),
              jnp.full((bsz, ML_HEADS), -jnp.inf, jnp.float32))
    _, hs = lax.scan(step, carry0, tuple(to_chunks(t, ML_CHUNK) for t in (q, k, v, li, lf)))
    return from_chunks(hs)


def mlstm_mixer(u, w_in, conv_w, conv_b, w_q, w_k, w_v, gate_b, skip, onorm, w_out):
    f32 = jnp.float32
    bsz, seq = u.shape[:2]
    xm, z, og, gates = jnp.split(u @ w_in, [D_INNER, 2 * D_INNER, 3 * D_INNER], axis=-1)
    ch = jax.nn.silu(dwconv(xm, conv_w, conv_b)).astype(f32).reshape(bsz, seq, ML_HEADS, ML_DH)
    xmh = xm.astype(f32).reshape(bsz, seq, ML_HEADS, ML_DH)
    q = jnp.einsum('blhd,hdk->blhk', ch, w_q.astype(f32))
    k = jnp.einsum('blhd,hdk->blhk', ch, w_k.astype(f32)) * (ML_DK ** -0.5)
    v = jnp.einsum('blhd,hde->blhe', xmh, w_v.astype(f32))
    g = gates.astype(f32).reshape(bsz, seq, 2, 2, ML_HEADS) + gate_b.astype(f32)
    li = g[:, :, :, 0]
    lf = jax.nn.log_sigmoid(g[:, :, :, 1])
    h = (mlstm_scan(q, k, v, li[:, :, 0], lf[:, :, 0])
         + rev(mlstm_scan(rev(q), rev(k), rev(v), rev(li[:, :, 1]), rev(lf[:, :, 1]))))
    h = jax.nn.sigmoid(og.astype(f32)).reshape(bsz, seq, ML_HEADS, ML_DH) * h
    h = rmsnorm(h, onorm) + skip.astype(f32) * ch
    h = h.reshape(bsz, seq, D_INNER) * jax.nn.silu(z.astype(f32))
    return h.astype(u.dtype) @ w_out


def setup_inputs(seed: int = 0) -> dict:
    key = jax.random.key(seed)
    keys = iter(jax.random.split(key, 64))
    f32 = jnp.float32

    def nrm(shape, scale):
        return scale * jax.random.normal(next(keys), shape, f32)

    def gain(shape):
        return 1.0 + nrm(shape, 0.02)

    na, nb, nh, nm = (len(range(t, DEPTH, N_MIXERS)) for t in range(N_MIXERS))
    w_in_s = D_MODEL ** -0.5
    w_out_s = D_INNER ** -0.5
    x = nrm((BATCH, SEQ, D_MODEL), 1.0)
    dt0 = jnp.exp(jax.random.uniform(next(keys), (na, 2, SSD_HEADS), f32, math.log(1e-3), math.log(1e-1)))
    ssd_dt_bias = dt0 + jnp.log(-jnp.expm1(-dt0))
    ssd_a_log = jnp.log(jax.random.uniform(next(keys), (na, 2, SSD_HEADS), f32, 1.0, 16.0))
    ml_gate_b = jnp.concatenate([nrm((nm, 2, 1, ML_HEADS), 0.1),
                                 jnp.linspace(3.0, 6.0, ML_HEADS, dtype=f32) + nrm((nm, 2, 1, ML_HEADS), 0.1)], axis=2)
    return {
        "x": x,
        "ssd_norm": gain((na, D_MODEL)),
        "ssd_w_in": nrm((na, D_MODEL, SSD_PROJ), w_in_s),
        "ssd_conv_w": nrm((na, SSD_CONV, SSD_CONV_DIM), SSD_CONV ** -0.5),
        "ssd_conv_b": nrm((na, SSD_CONV_DIM), 0.02),
        "ssd_dt_bias": ssd_dt_bias,
        "ssd_a_log": ssd_a_log,
        "ssd_d": gain((na, SSD_HEADS)),
        "ssd_gnorm": gain((na, D_INNER)),
        "ssd_w_out": nrm((na, D_INNER, D_MODEL), w_out_s),
        "gla_norm": gain((nb, D_MODEL)),
        "gla_w_in": nrm((nb, D_MODEL, GLA_PROJ), w_in_s),
        "gla_w_gate": nrm((nb, 2, GLA_RANK, GLA_QK), GLA_RANK ** -0.5),
        "gla_b_gate": nrm((nb, 2, GLA_QK), 0.02),
        "gla_onorm": gain((nb, GLA_DV)),
        "gla_w_out": nrm((nb, D_INNER, D_MODEL), w_out_s),
        "hy_norm": gain((nh, D_MODEL)),
        "hy_w_in": nrm((nh, D_MODEL, HY_PROJ), w_in_s),
        "hy_conv_w": nrm((nh, HY_SHORT, 3 * D_INNER), HY_SHORT ** -0.5),
        "hy_conv_b": nrm((nh, 3 * D_INNER), 0.02),
        "hy_ffn_w_in": nrm((nh, HY_EMB, HY_FFN), HY_EMB ** -0.5),
        "hy_ffn_b_in": nrm((nh, HY_FFN), 0.1),
        "hy_ffn_w_hid": nrm((nh, HY_INNER, HY_FFN, HY_FFN), HY_FFN ** -0.5),
        "hy_ffn_b_hid": nrm((nh, HY_INNER, HY_FFN), 0.1),
        "hy_ffn_freq": gain((nh, HY_INNER + 1, HY_FFN)),
        "hy_ffn_w_out": nrm((nh, HY_FFN, HY_FILT), 0.05 * HY_FFN ** -0.5),
        "hy_d": nrm((nh, HY_ORDER, D_INNER), 1.0),
        "hy_w_out": nrm((nh, D_INNER, D_MODEL), w_out_s),
        "ml_norm": gain((nm, D_MODEL)),
        "ml_w_in": nrm((nm, D_MODEL, ML_PROJ), w_in_s),
        "ml_conv_w": nrm((nm, ML_CONV, D_INNER), ML_CONV ** -0.5),
        "ml_conv_b": nrm((nm, D_INNER), 0.02),
        "ml_w_q": nrm((nm, ML_HEADS, ML_DH, ML_DK), ML_DH ** -0.5),
        "ml_w_k": nrm((nm, ML_HEADS, ML_DH, ML_DK), ML_DH ** -0.5),
        "ml_w_v": nrm((nm, ML_HEADS, ML_DH, ML_DH), ML_DH ** -0.5),
        "ml_gate_b": ml_gate_b,
        "ml_skip": gain((nm, ML_HEADS, ML_DH)),
        "ml_onorm": gain((nm, ML_DH)),
        "ml_w_out": nrm((nm, D_INNER, D_MODEL), w_out_s),
        "final_norm": gain((D_MODEL,)),
    }


def reference(x,
              ssd_norm, ssd_w_in, ssd_conv_w, ssd_conv_b, ssd_dt_bias, ssd_a_log, ssd_d, ssd_gnorm, ssd_w_out,
              gla_norm, gla_w_in, gla_w_gate, gla_b_gate, gla_onorm, gla_w_out,
              hy_norm, hy_w_in, hy_conv_w, hy_conv_b, hy_ffn_w_in, hy_ffn_b_in, hy_ffn_w_hid, hy_ffn_b_hid,
              hy_ffn_freq, hy_ffn_w_out, hy_d, hy_w_out,
              ml_norm, ml_w_in, ml_conv_w, ml_conv_b, ml_w_q, ml_w_k, ml_w_v, ml_gate_b, ml_skip, ml_onorm, ml_w_out,
              final_norm):
    h = x
    for i in range(DEPTH):
        kind, j = i % N_MIXERS, i // N_MIXERS
        if kind == 0:
            h = h + ssd_mixer(rmsnorm(h, ssd_norm[j]), ssd_w_in[j], ssd_conv_w[j], ssd_conv_b[j], ssd_dt_bias[j],
                              ssd_a_log[j], ssd_d[j], ssd_gnorm[j], ssd_w_out[j])
        elif kind == 1:
            h = h + gla_mixer(rmsnorm(h, gla_norm[j]), gla_w_in[j], gla_w_gate[j], gla_b_gate[j], gla_onorm[j],
                              gla_w_out[j])
        elif kind == 2:
            h = h + hyena_mixer(rmsnorm(h, hy_norm[j]), hy_w_in[j], hy_conv_w[j], hy_conv_b[j], hy_ffn_w_in[j],
                                hy_ffn_b_in[j], hy_ffn_w_hid[j], hy_ffn_b_hid[j], hy_ffn_freq[j], hy_ffn_w_out[j],
                                hy_d[j], hy_w_out[j])
        else:
            h = h + mlstm_mixer(rmsnorm(h, ml_norm[j]), ml_w_in[j], ml_conv_w[j], ml_conv_b[j], ml_w_q[j], ml_w_k[j],
                                ml_w_v[j], ml_gate_b[j], ml_skip[j], ml_onorm[j], ml_w_out[j])
    return rmsnorm(h, final_norm)
```

```python
import functools
import math

import jax
import jax.numpy as jnp
from jax import lax
from jax.experimental import pallas as pl
from jax.experimental.pallas import tpu as pltpu

F32 = jnp.float32
BF16 = jnp.bfloat16

D_MODEL = 1024
D_INNER = 2 * D_MODEL
EPS = 1e-6

SSD_HEAD_DIM = 64
SSD_HEADS = D_INNER // SSD_HEAD_DIM
SSD_GROUPS = 8
SSD_HPG = SSD_HEADS // SSD_GROUPS
SSD_STATE = 128
SSD_CHUNK = 128
SSD_GW = SSD_HPG * SSD_HEAD_DIM
SSD_BC = SSD_GROUPS * SSD_STATE

GLA_HEADS = 4
GLA_DK = D_MODEL // 2 // GLA_HEADS
GLA_DV = D_INNER // GLA_HEADS
GLA_RANK = 16
GLA_NORMALIZER = 16.0
GLA_CHUNK = 64
GLA_QK = GLA_HEADS * GLA_DK

HY_ORDER = 2
HY_EMB = 33
HY_BANDS = (HY_EMB - 1) // 2
HY_FFN = 64
HY_INNER = 2
HY_FAST_PCT = 0.3
HY_SLOW_PCT = 1.5
HY_TARGET = 1e-2

ML_HEADS = 4
ML_DH = D_INNER // ML_HEADS
ML_DK = ML_DH // 2
ML_CHUNK = 64

LANES = 128
VMEM_LIMIT = 56 * 1024 * 1024

_NEG_INF = float("-inf")


def _params(*sem):
    return pltpu.CompilerParams(dimension_semantics=sem, vmem_limit_bytes=VMEM_LIMIT)


def _sigmoid(x):
    return 1.0 / (1.0 + jnp.exp(-x))


def _silu(x):
    return x * _sigmoid(x)


def _softplus(x):
    return jnp.maximum(x, 0.0) + jnp.log1p(jnp.exp(-jnp.abs(x)))


def _log_sigmoid(x):
    return -_softplus(-x)


def _dot(a, b):
    return jnp.dot(a, b, preferred_element_type=F32)


def _dot_nt(a, b):
    return lax.dot_general(a, b, (((1,), (1,)), ((), ())), preferred_element_type=F32)


def _dot_tn(a, b):
    return lax.dot_general(a, b, (((0,), (0,)), ((), ())), preferred_element_type=F32)


def _dot_f32(a, b):
    return jnp.dot(a, b, preferred_element_type=F32, precision=lax.Precision.HIGHEST)


def _split(v):
    hi = v.astype(BF16)
    lo = (v - hi.astype(F32)).astype(BF16)
    return hi, lo


def _tri_dot(t, v):
    hi, lo = _split(v)
    return _dot(t, hi) + _dot(t, lo)


def _dot_tri(v, t):
    hi, lo = _split(v)
    return _dot(hi, t) + _dot(lo, t)


def _tri_consts(q):
    ri = lax.broadcasted_iota(jnp.int32, (q, q), 0)
    ci = lax.broadcasted_iota(jnp.int32, (q, q), 1)
    lower = ri >= ci
    upper = ci >= ri
    t_lower = jnp.where(lower, 1.0, 0.0).astype(BF16)
    t_upper = jnp.where(upper, 1.0, 0.0).astype(BF16)
    return (lower, upper), (t_lower, t_upper)


def _shift_rows(x, k):
    if k == 0:
        return x
    n = x.shape[0]
    rows = lax.broadcasted_iota(jnp.int32, x.shape, 0)
    y = pltpu.roll(x, k % n, axis=0)
    if k > 0:
        return jnp.where(rows >= k, y, 0.0)
    return jnp.where(rows < n + k, y, 0.0)


def _dwconv(x, w, b):
    taps = w.shape[0]
    pad = (taps - 1) // 2
    acc = x * w[pad:pad + 1, :] + b
    for j in range(taps):
        if j != pad:
            acc = acc + _shift_rows(x, pad - j) * w[j:j + 1, :]
    return acc


def _rms(x):
    return x * lax.rsqrt(jnp.mean(x * x, axis=-1, keepdims=True) + EPS)


def _inproj_kernel(x_ref, g_ref, w_ref, o_ref, xn_ref):
    @pl.when(pl.program_id(1) == 0)
    def _():
        xn_ref[...] = (_rms(x_ref[...]) * g_ref[...]).astype(BF16)

    o_ref[...] = _dot(xn_ref[...], w_ref[...]).astype(o_ref.dtype)


def _inproj(x2d, g, w, out_dtype, tn):
    m, k = x2d.shape
    n = w.shape[1]
    tm = min(1024, m)
    return pl.pallas_call(
        _inproj_kernel,
        grid=(m // tm, n // tn),
        in_specs=[pl.BlockSpec((tm, k), lambda i, j: (i, 0)),
                  pl.BlockSpec((1, k), lambda i, j: (0, 0)),
                  pl.BlockSpec((k, tn), lambda i, j: (0, j))],
        out_specs=pl.BlockSpec((tm, tn), lambda i, j: (i, j)),
        out_shape=jax.ShapeDtypeStruct((m, n), out_dtype),
        scratch_shapes=[pltpu.VMEM((tm, k), BF16)],
        compiler_params=_params("parallel", "arbitrary"),
        name="inproj",
    )(x2d, g.reshape(1, k).astype(F32), w)


def _outproj_kernel(y_ref, w_ref, r_ref, g_ref, o_ref, *, final):
    acc = _dot(y_ref[...], w_ref[...]) + r_ref[...]
    if final:
        acc = _rms(acc) * g_ref[...]
    o_ref[...] = acc


def _outproj(y2d, w, res2d, final_g=None):
    m, k = y2d.shape
    n = w.shape[1]
    tm = min(512, m)
    final = final_g is not None
    g = (final_g if final else jnp.ones((n,), F32)).reshape(1, n).astype(F32)
    return pl.pallas_call(
        functools.partial(_outproj_kernel, final=final),
        grid=(m // tm,),
        in_specs=[pl.BlockSpec((tm, k), lambda i: (i, 0)),
                  pl.BlockSpec((k, n), lambda i: (0, 0)),
                  pl.BlockSpec((tm, n), lambda i: (i, 0)),
                  pl.BlockSpec((1, n), lambda i: (0, 0))],
        out_specs=pl.BlockSpec((tm, n), lambda i: (i, 0)),
        out_shape=jax.ShapeDtypeStruct((m, n), F32),
        compiler_params=_params("parallel"),
        name="outproj",
    )(y2d, w.astype(BF16), res2d, g)


def _pad_cols(w, n):
    return jnp.pad(w, ((0, 0), (0, n - w.shape[1])))


def _ssd_kernel(z_ref, x_ref, b_ref, c_ref, dtc_ref, dtr_ref,
                cwx_ref, cwb_ref, cwc_ref, cbx_ref, cbb_ref, cbc_ref,
                biasr_ref, biasc_ref, alogr_ref, alogc_ref, dskip_ref, gnorm_ref,
                o_ref, xs_ref, cs_ref, bt_ref, y_ref, s_ref):
    q = SSD_CHUNK
    seq = x_ref.shape[0]
    nc = seq // q
    masks, tris = _tri_consts(q)
    lane_head = lax.shift_right_logical(
        lax.broadcasted_iota(jnp.int32, (1, SSD_GW), 1), int(math.log2(SSD_HEAD_DIM)))

    xs_ref[...] = _silu(_dwconv(x_ref[...].astype(F32), cwx_ref[...], cbx_ref[...]))
    cs_ref[...] = _silu(_dwconv(c_ref[...].astype(F32), cwc_ref[...], cbc_ref[...])).astype(BF16)
    bs = _silu(_dwconv(b_ref[...].astype(F32), cwb_ref[...], cbb_ref[...]))
    for c in range(nc):
        bt_ref[c] = bs[c * q:(c + 1) * q, :].T.astype(BF16)

    bias_row = biasr_ref[...]
    bias_col = biasc_ref[...]
    a_row = -jnp.exp(alogr_ref[...])
    a_col = -jnp.exp(alogc_ref[...])

    def expand(cols):
        out = jnp.broadcast_to(cols[:, SSD_HPG - 1:SSD_HPG], (q, SSD_GW))
        for r in range(SSD_HPG - 2, -1, -1):
            out = jnp.where(lane_head == r, cols[:, r:r + 1], out)
        return out

    for d in range(2):
        mask, tri, tri_t = masks[d], tris[d], tris[1 - d]
        s_ref[...] = jnp.zeros_like(s_ref)

        def chunk(i, carry, d=d, mask=mask, tri=tri, tri_t=tri_t):
            c = i if d == 0 else nc - 1 - i
            r0 = pl.multiple_of(c * q, q)
            xq = xs_ref[pl.ds(r0, q), :]
            cq = cs_ref[pl.ds(r0, q), :]
            bt = bt_ref[c]
            dt_col = _softplus(dtc_ref[pl.ds(r0, q), :] + bias_row)
            dt_row = _softplus(dtr_ref[c] + bias_col)
            la_col = dt_col * a_row
            la_row = dt_row * a_col
            lo = d * SSD_HPG
            dt_exp = expand(dt_col[:, lo:lo + SSD_HPG])
            cum_exp = _tri_dot(tri, expand(la_col[:, lo:lo + SSD_HPG]))
            cum_row = _dot_tri(la_row, tri_t)
            cb = _dot(cq, bt)
            ms = []
            for r in range(SSD_HPG):
                col = cum_exp[:, r * SSD_HEAD_DIM:r * SSD_HEAD_DIM + 1]
                row = cum_row[lo + r:lo + r + 1, :]
                dec = jnp.exp(jnp.where(mask, col - row, _NEG_INF))
                ms.append((cb * dec * dt_row[lo + r:lo + r + 1, :]).astype(BF16))
            mcat = jnp.concatenate(ms, axis=1)
            xb = xq.astype(BF16)
            xbd = jnp.concatenate(
                [jnp.where(lane_head == r, xb, jnp.zeros_like(xb)) for r in range(SSD_HPG)], axis=0)
            tot = cum_exp[q - 1:q, :] if d == 0 else cum_exp[0:1, :]
            state = s_ref[...]
            y = _dot(mcat, xbd) + _dot(cq, state.astype(BF16)) * jnp.exp(cum_exp)
            xw = (xq * dt_exp * jnp.exp(tot - cum_exp)).astype(BF16)
            s_ref[...] = state * jnp.exp(tot) + _dot(bt, xw)
            if d == 0:
                y_ref[pl.ds(r0, q), :] = y
            else:
                y_ref[pl.ds(r0, q), :] += y
            return carry

        lax.fori_loop(0, nc, chunk, 0)

    y = y_ref[...] + xs_ref[...] * dskip_ref[...]
    y = y * _silu(z_ref[...].astype(F32))
    o_ref[...] = (_rms(y) * gnorm_ref[...]).astype(o_ref.dtype)


def _ssd_layer(h, norm, w_in, conv_w, conv_b, dt_bias, a_log, d_skip, gnorm, w_out, final_g):
    bsz, seq, _ = h.shape
    g, hpg, q = SSD_GROUPS, SSD_HPG, SSD_CHUNK
    nc = seq // q
    h2 = h.reshape(bsz * seq, D_MODEL)
    n_main = 2 * D_INNER + 2 * SSD_BC
    pm = _inproj(h2, norm, w_in[:, :n_main].astype(BF16), BF16, 1024).reshape(bsz, seq, n_main)
    gates = _inproj(h2, norm, _pad_cols(w_in[:, n_main:], LANES).astype(BF16), F32, LANES)
    dt = gates[:, :2 * SSD_HEADS].reshape(bsz, seq, 2, g, hpg)
    dt_col = dt.transpose(0, 3, 1, 2, 4).reshape(bsz, g, seq, 2 * hpg)
    dt_row = dt.reshape(bsz, nc, q, 2, g, hpg).transpose(0, 4, 1, 3, 5, 2).reshape(bsz, g, nc, 2 * hpg, q)

    def per_group(p):
        return p.astype(F32).reshape(2, g, hpg).transpose(1, 0, 2).reshape(g, 2 * hpg)

    bias, alog = per_group(dt_bias), per_group(a_log)
    dskip = jnp.repeat(d_skip.astype(F32).reshape(g, hpg), SSD_HEAD_DIM, axis=1).reshape(g, 1, SSD_GW)
    cw = conv_w.astype(F32)
    cb = conv_b.astype(F32).reshape(1, -1)
    taps = cw.shape[0]
    xo = D_INNER // SSD_GW
    bo = 2 * D_INNER // SSD_STATE
    co = bo + SSD_GROUPS
    cbo = D_INNER // SSD_STATE
    cco = cbo + SSD_GROUPS

    y = pl.pallas_call(
        _ssd_kernel,
        grid=(bsz, g),
        in_specs=[
            pl.BlockSpec((None, seq, SSD_GW), lambda b, j: (b, 0, j)),
            pl.BlockSpec((None, seq, SSD_GW), lambda b, j: (b, 0, xo + j)),
            pl.BlockSpec((None, seq, SSD_STATE), lambda b, j: (b, 0, bo + j)),
            pl.BlockSpec((None, seq, SSD_STATE), lambda b, j: (b, 0, co + j)),
            pl.BlockSpec((None, None, seq, 2 * hpg), lambda b, j: (b, j, 0, 0)),
            pl.BlockSpec((None, None, nc, 2 * hpg, q), lambda b, j: (b, j, 0, 0, 0)),
            pl.BlockSpec((taps, SSD_GW), lambda b, j: (0, j)),
            pl.BlockSpec((taps, SSD_STATE), lambda b, j: (0, cbo + j)),
            pl.BlockSpec((taps, SSD_STATE), lambda b, j: (0, cco + j)),
            pl.BlockSpec((1, SSD_GW), lambda b, j: (0, j)),
            pl.BlockSpec((1, SSD_STATE), lambda b, j: (0, cbo + j)),
            pl.BlockSpec((1, SSD_STATE), lambda b, j: (0, cco + j)),
            pl.BlockSpec((None, 1, 2 * hpg), lambda b, j: (j, 0, 0)),
            pl.BlockSpec((None, 2 * hpg, 1), lambda b, j: (j, 0, 0)),
            pl.BlockSpec((None, 1, 2 * hpg), lambda b, j: (j, 0, 0)),
            pl.BlockSpec((None, 2 * hpg, 1), lambda b, j: (j, 0, 0)),
            pl.BlockSpec((None, 1, SSD_GW), lambda b, j: (j, 0, 0)),
            pl.BlockSpec((None, 1, SSD_GW), lambda b, j: (j, 0, 0)),
        ],
        out_specs=pl.BlockSpec((None, seq, SSD_GW), lambda b, j: (b, 0, j)),
        out_shape=jax.ShapeDtypeStruct((bsz, seq, D_INNER), BF16),
        scratch_shapes=[pltpu.VMEM((seq, SSD_GW), F32),
                        pltpu.VMEM((seq, SSD_STATE), BF16),
                        pltpu.VMEM((nc, SSD_STATE, q), BF16),
                        pltpu.VMEM((seq, SSD_GW), F32),
                        pltpu.VMEM((SSD_STATE, SSD_GW), F32)],
        compiler_params=_params("parallel", "parallel"),
        name="ssd_mixer",
    )(pm, pm, pm, pm, dt_col, dt_row, cw, cw, cw, cb, cb, cb,
      bias.reshape(g, 1, 2 * hpg), bias.reshape(g, 2 * hpg, 1),
      alog.reshape(g, 1, 2 * hpg), alog.reshape(g, 2 * hpg, 1),
      dskip, gnorm.astype(F32).reshape(g, 1, SSD_GW))
    out = _outproj(y.reshape(bsz * seq, D_INNER), w_out, h2, final_g)
    return out.reshape(bsz, seq, D_MODEL)


def _gla_kernel(q_ref, k_ref, v_ref, z_ref, gl_ref, wg_ref, bg_ref, onorm_ref,
                o_ref, lg_ref, acc_ref, s_ref):
    q = GLA_CHUNK
    seq = q_ref.shape[0]
    nc = seq // q
    masks, tris = _tri_consts(q)
    gl = gl_ref[...]
    for d in range(2):
        lg_ref[d] = _log_sigmoid(_dot_f32(gl, wg_ref[d]) + bg_ref[d]) * (1.0 / GLA_NORMALIZER)

    for d in range(2):
        mask, tri = masks[d], tris[d]
        s_ref[...] = jnp.zeros_like(s_ref)

        def chunk(i, carry, d=d, mask=mask, tri=tri):
            c = i if d == 0 else nc - 1 - i
            r0 = pl.multiple_of(c * q, q)
            qc = q_ref[pl.ds(r0, q), :].astype(F32) * (GLA_DK ** -0.5)
            kc = k_ref[pl.ds(r0, q), :].astype(F32)
            vc = v_ref[pl.ds(r0, q), :]
            cum = _tri_dot(tri, lg_ref[d, pl.ds(r0, q), :])
            tot = cum[q - 1:q, :] if d == 0 else cum[0:1, :]
            qg = (qc * jnp.exp(cum)).astype(BF16)
            kg = (kc * jnp.exp(-cum)).astype(BF16)
            kd = (kc * jnp.exp(tot - cum)).astype(BF16)
            att = jnp.where(mask, _dot_nt(qg, kg), 0.0).astype(BF16)
            state = s_ref[...]
            out = _dot(att, vc) + _dot_nt(qg, state.astype(BF16))
            s_ref[...] = state * jnp.exp(tot) + _dot_tn(vc, kd)
            if d == 0:
                acc_ref[pl.ds(r0, q), :] = out
            else:
                acc_ref[pl.ds(r0, q), :] += out
            return carry

        lax.fori_loop(0, nc, chunk, 0)

    out = _rms(acc_ref[...]) * onorm_ref[...]
    o_ref[...] = (out * _silu(z_ref[...].astype(F32))).astype(o_ref.dtype)


def _gla_layer(h, norm, w_in, w_gate, b_gate, onorm, w_out, final_g):
    bsz, seq, _ = h.shape
    nh = GLA_HEADS
    h2 = h.reshape(bsz * seq, D_MODEL)
    n_main = 2 * GLA_QK + 2 * D_INNER
    pm = _inproj(h2, norm, w_in[:, :n_main].astype(BF16), BF16, 1024).reshape(bsz, seq, n_main)
    gates = _inproj(h2, norm, _pad_cols(w_in[:, n_main:], LANES).astype(BF16), F32, LANES)
    gates = gates.reshape(bsz, seq, LANES)
    wg = jnp.zeros((2, LANES, GLA_QK), F32)
    for d in range(2):
        wg = wg.at[d, d * GLA_RANK:(d + 1) * GLA_RANK, :].set(w_gate[d].astype(F32))
    vo = 2 * GLA_QK // GLA_DV
    zo = vo + nh
    y = pl.pallas_call(
        _gla_kernel,
        grid=(bsz, nh),
        in_specs=[
            pl.BlockSpec((None, seq, GLA_DK), lambda b, j: (b, 0, j)),
            pl.BlockSpec((None, seq, GLA_DK), lambda b, j: (b, 0, nh + j)),
            pl.BlockSpec((None, seq, GLA_DV), lambda b, j: (b, 0, vo + j)),
            pl.BlockSpec((None, seq, GLA_DV), lambda b, j: (b, 0, zo + j)),
            pl.BlockSpec((None, seq, LANES), lambda b, j: (b, 0, 0)),
            pl.BlockSpec((2, LANES, GLA_DK), lambda b, j: (0, 0, j)),
            pl.BlockSpec((2, 1, GLA_DK), lambda b, j: (0, 0, j)),
            pl.BlockSpec((1, GLA_DV), lambda b, j: (0, 0)),
        ],
        out_specs=pl.BlockSpec((None, seq, GLA_DV), lambda b, j: (b, 0, j)),
        out_shape=jax.ShapeDtypeStruct((bsz, seq, D_INNER), BF16),
        scratch_shapes=[pltpu.VMEM((2, seq, GLA_DK), F32),
                        pltpu.VMEM((seq, GLA_DV), F32),
                        pltpu.VMEM((GLA_DV, GLA_DK), F32)],
        compiler_params=_params("parallel", "parallel"),
        name="gla_mixer",
    )(pm, pm, pm, pm, gates, wg, b_gate.astype(F32).reshape(2, 1, GLA_QK),
      onorm.astype(F32).reshape(1, GLA_DV))
    out = _outproj(y.reshape(bsz * seq, D_INNER), w_out, h2, final_g)
    return out.reshape(bsz, seq, D_MODEL)


def _hy_mlp_kernel(f_ref, w0_ref, b0_ref, w1_ref, b1_ref, w2_ref, b2_ref, fr_ref, o_ref):
    fr = fr_ref[...]
    hid = jnp.sin(fr[0:1, :] * (_dot_f32(f_ref[...], w0_ref[...]) + b0_ref[...]))
    hid = jnp.sin(fr[1:2, :] * (_dot_f32(hid, w1_ref[...]) + b1_ref[...]))
    hid = jnp.sin(fr[2:3, :] * (_dot_f32(hid, w2_ref[...]) + b2_ref[...]))
    o_ref[...] = hid


def _hy_spec_kernel(hid_ref, wf_ref, wb_ref, t_ref, dl_ref, cos_ref, sin_ref,
                    kre_ref, ks_ref, a_ref, b_ref):
    seq = hid_ref.shape[0]

    @pl.when(pl.program_id(2) == 0)
    def _():
        env = jnp.exp(-t_ref[...] * dl_ref[...])
        hf = _dot_f32(hid_ref[...], wf_ref[...]) * env
        hb = _dot_f32(hid_ref[...], wb_ref[...]) * env
        rows = lax.broadcasted_iota(jnp.int32, hb.shape, 0)
        hb = jnp.where(rows == 0, 0.0, hb)
        a_ref[...] = (hf + hb).astype(BF16)
        b_ref[...] = (hf - hb).astype(BF16)

    scale = 1.0 / seq
    kre_ref[...] = (_dot(cos_ref[...], a_ref[...]) * scale).astype(kre_ref.dtype)
    ks_ref[...] = (_dot(sin_ref[...], b_ref[...]) * scale).astype(ks_ref.dtype)


def _hy_conv_kernel(v_ref, x1_ref, x2_ref, z_ref, cwv_ref, cw1_ref, cw2_ref,
                    cbv_ref, cb1_ref, cb2_ref, kre_ref, ks_ref, d_ref, w_ref, o_ref,
                    y_ref, yb_ref, zf_ref, g_ref, *, fb):
    seq = v_ref.shape[0]
    nblk = seq // fb
    y_ref[...] = _dwconv(v_ref[...].astype(F32), cwv_ref[...], cbv_ref[...])
    gates = ((x1_ref, cw1_ref, cb1_ref), (x2_ref, cw2_ref, cb2_ref))
    for o, (x_ref, cw_ref, cb_ref) in enumerate(gates):
        yb_ref[...] = y_ref[...].astype(BF16)
        g_ref[...] = _dwconv(x_ref[...].astype(F32), cw_ref[...], cb_ref[...])

        def forward(k, carry, o=o):
            f0 = pl.multiple_of(k * fb, fb)
            yb = yb_ref[...]
            yre = _dot(w_ref[pl.ds(f0, fb), :], yb)
            yim = _dot(w_ref[pl.ds(seq + f0, fb), :], yb)
            kre = kre_ref[o, pl.ds(f0, fb), :].astype(F32)
            ks = ks_ref[o, pl.ds(f0, fb), :].astype(F32)
            zf_ref[pl.ds(f0, fb), :] = (yre * kre - yim * ks).astype(BF16)
            zf_ref[pl.ds(seq + f0, fb), :] = (yim * kre + yre * ks).astype(BF16)
            return carry

        lax.fori_loop(0, nblk, forward, 0)

        def inverse(k, carry, o=o):
            t0 = pl.multiple_of(k * fb, fb)
            conv = (_dot(w_ref[pl.ds(t0, fb), :], zf_ref[0:seq, :])
                    + _dot(w_ref[pl.ds(seq + t0, fb), :], zf_ref[seq:2 * seq, :]))
            rows = pl.ds(t0, fb)
            y_ref[rows, :] = g_ref[rows, :] * (conv + y_ref[rows, :] * d_ref[o:o + 1, :])
            return carry

        lax.fori_loop(0, nblk, inverse, 0)
    o_ref[...] = (y_ref[...] * _silu(z_ref[...].astype(F32))).astype(o_ref.dtype)


def _hyena_layer(h, norm, w_in, conv_w, conv_b, ffn_w_in, ffn_b_in, ffn_w_hid, ffn_b_hid,
                 ffn_freq, ffn_w_out, d_bias, w_out, final_g):
    bsz, seq, _ = h.shape
    c = D_INNER
    h2 = h.reshape(bsz * seq, D_MODEL)
    pm = _inproj(h2, norm, w_in.astype(BF16), BF16, 1024).reshape(bsz, seq, 4 * c)

    t = jnp.linspace(0.0, 1.0, seq, dtype=F32)[:, None]
    pos = jnp.arange(seq, dtype=F32)[:, None]
    bands = jnp.linspace(1e-4, HY_BANDS - 1, HY_BANDS, dtype=F32)[None]
    ang = (2.0 * math.pi / seq) * pos * bands
    feats = jnp.concatenate([t, jnp.cos(ang), -jnp.sin(ang)], axis=-1)
    feats = _pad_cols(feats, LANES)
    w0 = jnp.pad(ffn_w_in.astype(F32), ((0, LANES - HY_EMB), (0, 0)))
    hid = pl.pallas_call(
        _hy_mlp_kernel,
        out_shape=jax.ShapeDtypeStruct((seq, HY_FFN), F32),
        name="hyena_filter_mlp",
    )(feats, w0, ffn_b_in.astype(F32).reshape(1, HY_FFN),
      ffn_w_hid[0].astype(F32), ffn_b_hid[0].astype(F32).reshape(1, HY_FFN),
      ffn_w_hid[1].astype(F32), ffn_b_hid[1].astype(F32).reshape(1, HY_FFN),
      ffn_freq.astype(F32))

    max_decay = math.log(HY_TARGET) / HY_FAST_PCT
    min_decay = math.log(HY_TARGET) / HY_SLOW_PCT
    deltas = jnp.abs(jnp.linspace(min_decay, max_decay, c, dtype=F32)).reshape(1, c)

    fi = jnp.arange(seq, dtype=jnp.int32)
    n4 = 8 * seq

    def table(num, fn):
        return fn((2.0 * math.pi / n4) * (num % n4).astype(F32)).astype(BF16)

    num1 = (2 * fi[:, None] + 1) * (2 * fi[None, :])
    num2 = (2 * fi[:, None] + 1) * (2 * fi[None, :] + 1)
    cos1, sin1 = table(num1, jnp.cos), table(num1, jnp.sin)
    w2 = jnp.concatenate([table(num2, jnp.cos), table(num2, jnp.sin)], axis=0)

    tc = min(512, c)
    fb = min(512, seq)
    nct = c // tc
    wo = ffn_w_out.astype(F32)
    kre, ks = pl.pallas_call(
        _hy_spec_kernel,
        grid=(HY_ORDER, nct, seq // fb),
        in_specs=[
            pl.BlockSpec((seq, HY_FFN), lambda o, j, k: (0, 0)),
            pl.BlockSpec((HY_FFN, tc), lambda o, j, k: (0, o * nct + j)),
            pl.BlockSpec((HY_FFN, tc), lambda o, j, k: (0, (HY_ORDER + o) * nct + j)),
            pl.BlockSpec((seq, 1), lambda o, j, k: (0, 0)),
            pl.BlockSpec((1, tc), lambda o, j, k: (0, j)),
            pl.BlockSpec((fb, seq), lambda o, j, k: (k, 0)),
            pl.BlockSpec((fb, seq), lambda o, j, k: (k, 0)),
        ],
        out_specs=[pl.BlockSpec((None, fb, tc), lambda o, j, k: (o, k, j)),
                   pl.BlockSpec((None, fb, tc), lambda o, j, k: (o, k, j))],
        out_shape=[jax.ShapeDtypeStruct((HY_ORDER, seq, c), BF16)] * 2,
        scratch_shapes=[pltpu.VMEM((seq, tc), BF16), pltpu.VMEM((seq, tc), BF16)],
        compiler_params=_params("parallel", "parallel", "arbitrary"),
        name="hyena_filter_spectrum",
    )(hid, wo, wo, t, deltas, cos1, sin1)

    tcc = min(256, c)
    ncc = c // tcc
    cw = conv_w.astype(F32)
    cb = conv_b.astype(F32).reshape(1, -1)
    taps = cw.shape[0]
    y = pl.pallas_call(
        functools.partial(_hy_conv_kernel, fb=fb),
        grid=(ncc, bsz),
        in_specs=[
            pl.BlockSpec((None, seq, tcc), lambda j, b: (b, 0, j)),
            pl.BlockSpec((None, seq, tcc), lambda j, b: (b, 0, ncc + j)),
            pl.BlockSpec((None, seq, tcc), lambda j, b: (b, 0, 2 * ncc + j)),
            pl.BlockSpec((None, seq, tcc), lambda j, b: (b, 0, 3 * ncc + j)),
            pl.BlockSpec((taps, tcc), lambda j, b: (0, j)),
            pl.BlockSpec((taps, tcc), lambda j, b: (0, ncc + j)),
            pl.BlockSpec((taps, tcc), lambda j, b: (0, 2 * ncc + j)),
            pl.BlockSpec((1, tcc), lambda j, b: (0, j)),
            pl.BlockSpec((1, tcc), lambda j, b: (0, ncc + j)),
            pl.BlockSpec((1, tcc), lambda j, b: (0, 2 * ncc + j)),
            pl.BlockSpec((HY_ORDER, seq, tcc), lambda j, b: (0, 0, j), pipeline_mode=pl.Buffered(1)),
            pl.BlockSpec((HY_ORDER, seq, tcc), lambda j, b: (0, 0, j), pipeline_mode=pl.Buffered(1)),
            pl.BlockSpec((HY_ORDER, tcc), lambda j, b: (0, j)),
            pl.BlockSpec((2 * seq, seq), lambda j, b: (0, 0), pipeline_mode=pl.Buffered(1)),
        ],
        out_specs=pl.BlockSpec((None, seq, tcc), lambda j, b: (b, 0, j)),
        out_shape=jax.ShapeDtypeStruct((bsz, seq, c), BF16),
        scratch_shapes=[pltpu.VMEM((seq, tcc), F32),
                        pltpu.VMEM((seq, tcc), BF16),
                        pltpu.VMEM((2 * seq, tcc), BF16),
                        pltpu.VMEM((seq, tcc), F32)],
        compiler_params=_params("parallel", "parallel"),
        name="hyena_mixer",
    )(pm, pm, pm, pm, cw, cw, cw, cb, cb, cb, kre, ks, d_bias.astype(F32), w2)
    out = _outproj(y.reshape(bsz * seq, c), w_out, h2, final_g)
    return out.reshape(bsz, seq, D_MODEL)


def _mlstm_kernel(xm_ref, z_ref, og_ref, gc_ref, gr_ref, cw_ref, cb_ref, wq_ref, wk_ref, wv_ref,
                  gbr_ref, gbc_ref, skip_ref, onorm_ref,
                  o_ref, q_ref, k_ref, v_ref, ch_ref, acc_ref, c_ref):
    q = ML_CHUNK
    seq = xm_ref.shape[0]
    nc = seq // q
    masks, tris = _tri_consts(q)

    for j in range(ML_DH // LANES):
        sl = slice(j * LANES, (j + 1) * LANES)
        ch_ref[:, sl] = _silu(_dwconv(xm_ref[:, sl].astype(F32), cw_ref[:, sl], cb_ref[:, sl])).astype(BF16)
    ch = ch_ref[...]
    q_ref[...] = _dot(ch, wq_ref[...]).astype(BF16)
    k_ref[...] = (_dot(ch, wk_ref[...]) * (ML_DK ** -0.5)).astype(BF16)
    v_ref[...] = _dot(xm_ref[...], wv_ref[...]).astype(BF16)

    gb_row = gbr_ref[...]
    gb_col = gbc_ref[...]

    for d in range(2):
        mask, tri, tri_t = masks[d], tris[d], tris[1 - d]
        c_ref[...] = jnp.zeros_like(c_ref)

        def chunk(i, carry, d=d, mask=mask, tri=tri, tri_t=tri_t):
            n_row, m = carry
            c = i if d == 0 else nc - 1 - i
            r0 = pl.multiple_of(c * q, q)
            qc = q_ref[pl.ds(r0, q), :]
            kc = k_ref[pl.ds(r0, q), :]
            vc = v_ref[pl.ds(r0, q), :]
            g_col = gc_ref[pl.ds(r0, q), :] + gb_row
            g_row = gr_ref[c] + gb_col
            i_col = g_col[:, 2 * d:2 * d + 1]
            i_row = g_row[2 * d:2 * d + 1, :]
            b_col = _tri_dot(tri, _log_sigmoid(g_col))[:, 2 * d + 1:2 * d + 2]
            b_row = _dot_tri(_log_sigmoid(g_row), tri_t)[2 * d + 1:2 * d + 2, :]
            logd = jnp.where(mask, b_col - b_row + i_row, _NEG_INF)
            inter = b_col + m
            m_row = jnp.maximum(jnp.max(logd, axis=1, keepdims=True), inter)
            s = _dot_nt(qc, kc) * jnp.exp(logd - m_row)
            scale = jnp.exp(inter - m_row)
            cstate = c_ref[...]
            num = _dot(s.astype(BF16), vc) + _dot(qc, cstate.astype(BF16)) * scale
            den = (jnp.sum(s, axis=1, keepdims=True)
                   + jnp.sum(qc.astype(F32) * n_row, axis=1, keepdims=True) * scale)
            hc = num / jnp.maximum(jnp.abs(den), jnp.exp(-m_row))
            tot = b_col[q - 1:q, :] if d == 0 else b_col[0:1, :]
            lw_col = tot - b_col + i_col
            m_new = jnp.maximum(tot + m, jnp.max(lw_col, axis=0, keepdims=True))
            w_col = jnp.exp(lw_col - m_new)
            dec = jnp.exp(tot + m - m_new)
            wk = kc.astype(F32) * w_col
            c_ref[...] = cstate * dec + _dot_tn(wk.astype(BF16), vc)
            n_row = n_row * dec + jnp.sum(wk, axis=0, keepdims=True)
            if d == 0:
                acc_ref[pl.ds(r0, q), :] = hc
            else:
                acc_ref[pl.ds(r0, q), :] += hc
            return n_row, m_new

        lax.fori_loop(0, nc, chunk, (jnp.zeros((1, ML_DK), F32), jnp.full((1, 1), _NEG_INF, F32)))

    hh = _sigmoid(og_ref[...].astype(F32)) * acc_ref[...]
    hh = _rms(hh) * onorm_ref[...] + skip_ref[...] * ch_ref[...].astype(F32)
    o_ref[...] = (hh * _silu(z_ref[...].astype(F32))).astype(o_ref.dtype)


def _mlstm_layer(h, norm, w_in, conv_w, conv_b, w_q, w_k, w_v, gate_b, skip, onorm, w_out, final_g):
    bsz, seq, _ = h.shape
    nh, q = ML_HEADS, ML_CHUNK
    nc = seq // q
    h2 = h.reshape(bsz * seq, D_MODEL)
    n_main = 3 * D_INNER
    pm = _inproj(h2, norm, w_in[:, :n_main].astype(BF16), BF16, 1024).reshape(bsz, seq, n_main)
    gates = _inproj(h2, norm, _pad_cols(w_in[:, n_main:], LANES).astype(BF16), F32, LANES)
    gt = gates[:, :4 * nh].reshape(bsz, seq, 2, 2, nh)
    g_col = gt.transpose(0, 4, 1, 2, 3).reshape(bsz, nh, seq, 4)
    g_row = gt.reshape(bsz, nc, q, 2, 2, nh).transpose(0, 5, 1, 3, 4, 2).reshape(bsz, nh, nc, 4, q)
    g_row = jnp.pad(g_row, ((0, 0), (0, 0), (0, 0), (0, 4), (0, 0)))
    gb =gate_b.astype(F32).transpose(2, 0, 1).reshape(nh, 4)
    taps = conv_w.shape[0]
    y = pl.pallas_call(
        _mlstm_kernel,
        grid=(bsz, nh),
        in_specs=[
            pl.BlockSpec((None, seq, ML_DH), lambda b, j: (b, 0, j)),
            pl.BlockSpec((None, seq, ML_DH), lambda b, j: (b, 0, nh + j)),
            pl.BlockSpec((None, seq, ML_DH), lambda b, j: (b, 0, 2 * nh + j)),
            pl.BlockSpec((None, None, seq, 4), lambda b, j: (b, j, 0, 0)),
            pl.BlockSpec((None, None, nc, 8, q), lambda b, j: (b, j, 0, 0, 0)),
            pl.BlockSpec((taps, ML_DH), lambda b, j: (0, j)),
            pl.BlockSpec((1, ML_DH), lambda b, j: (0, j)),
            pl.BlockSpec((None, ML_DH, ML_DK), lambda b, j: (j, 0, 0)),
            pl.BlockSpec((None, ML_DH, ML_DK), lambda b, j: (j, 0, 0)),
            pl.BlockSpec((None, ML_DH, ML_DH), lambda b, j: (j, 0, 0)),
            pl.BlockSpec((None, 1, 4), lambda b, j: (j, 0, 0)),
            pl.BlockSpec((None, 8, 1), lambda b, j: (j, 0, 0)),
            pl.BlockSpec((None, 1, ML_DH), lambda b, j: (j, 0, 0)),
            pl.BlockSpec((1, ML_DH), lambda b, j: (0, 0)),
        ],
        out_specs=pl.BlockSpec((None, seq, ML_DH), lambda b, j: (b, 0, j)),
        out_shape=jax.ShapeDtypeStruct((bsz, seq, D_INNER), BF16),
        scratch_shapes=[pltpu.VMEM((seq, ML_DK), BF16),
                        pltpu.VMEM((seq, ML_DK), BF16),
                        pltpu.VMEM((seq, ML_DH), BF16),
                        pltpu.VMEM((seq, ML_DH), BF16),
                        pltpu.VMEM((seq, ML_DH), F32),
                        pltpu.VMEM((ML_DK, ML_DH), F32)],
        compiler_params=_params("parallel", "parallel"),
        name="mlstm_mixer",
    )(pm, pm, pm, g_col, g_row, conv_w.astype(F32), conv_b.astype(F32).reshape(1, -1),
      w_q.astype(BF16), w_k.astype(BF16), w_v.astype(BF16),
      gb.reshape(nh, 1, 4), jnp.pad(gb, ((0, 0), (0, 4))).reshape(nh, 8, 1),
      skip.astype(F32).reshape(nh, 1, ML_DH), onorm.astype(F32).reshape(1, ML_DH))
    out = _outproj(y.reshape(bsz * seq, D_INNER), w_out, h2, final_g)
    return out.reshape(bsz, seq, D_MODEL)


def kernel(x, ssd_norm, ssd_w_in, ssd_conv_w, ssd_conv_b, ssd_dt_bias, ssd_a_log, ssd_d, ssd_gnorm, ssd_w_out, gla_norm, gla_w_in, gla_w_gate, gla_b_gate, gla_onorm, gla_w_out, hy_norm, hy_w_in, hy_conv_w, hy_conv_b, hy_ffn_w_in, hy_ffn_b_in, hy_ffn_w_hid, hy_ffn_b_hid, hy_ffn_freq, hy_ffn_w_out, hy_d, hy_w_out, ml_norm, ml_w_in, ml_conv_w, ml_conv_b, ml_w_q, ml_w_k, ml_w_v, ml_gate_b, ml_skip, ml_onorm, ml_w_out, final_norm):
    depth = ssd_norm.shape[0] + gla_norm.shape[0] + hy_norm.shape[0] + ml_norm.shape[0]
    h = x
    for i in range(depth):
        kind, j = i % 4, i // 4
        fg = final_norm if i == depth - 1 else None
        if kind == 0:
            h = _ssd_layer(h, ssd_norm[j], ssd_w_in[j], ssd_conv_w[j], ssd_conv_b[j], ssd_dt_bias[j],
                           ssd_a_log[j], ssd_d[j], ssd_gnorm[j], ssd_w_out[j], fg)
        elif kind == 1:
            h = _gla_layer(h, gla_norm[j], gla_w_in[j], gla_w_gate[j], gla_b_gate[j], gla_onorm[j],
                           gla_w_out[j], fg)
        elif kind == 2:
            h = _hyena_layer(h, hy_norm[j], hy_w_in[j], hy_conv_w[j], hy_conv_b[j], hy_ffn_w_in[j],
                             hy_ffn_b_in[j], hy_ffn_w_hid[j], hy_ffn_b_hid[j], hy_ffn_freq[j],
                             hy_ffn_w_out[j], hy_d[j], hy_w_out[j], fg)
        else:
            h = _mlstm_layer(h, ml_norm[j], ml_w_in[j], ml_conv_w[j], ml_conv_b[j], ml_w_q[j], ml_w_k[j],
                             ml_w_v[j], ml_gate_b[j], ml_skip[j], ml_onorm[j], ml_w_out[j], fg)
    return h
```

```python
import functools
import math

import jax
import jax.numpy as jnp
from jax import lax
from jax.experimental import pallas as pl
from jax.experimental.pallas import tpu as pltpu

F32 = jnp.float32
BF16 = jnp.bfloat16

D_MODEL = 1024
D_INNER = 2 * D_MODEL
EPS = 1e-6

SSD_HEAD_DIM = 64
SSD_HEADS = D_INNER // SSD_HEAD_DIM
SSD_GROUPS = 8
SSD_HPG = SSD_HEADS // SSD_GROUPS
SSD_STATE = 128
SSD_CHUNK = 128
SSD_GW = SSD_HPG * SSD_HEAD_DIM
SSD_BC = SSD_GROUPS * SSD_STATE

GLA_HEADS = 4
GLA_DK = D_MODEL // 2 // GLA_HEADS
GLA_DV = D_INNER // GLA_HEADS
GLA_RANK = 16
GLA_NORMALIZER = 16.0
GLA_CHUNK = 64
GLA_QK = GLA_HEADS * GLA_DK

HY_ORDER = 2
HY_EMB = 33
HY_BANDS = (HY_EMB - 1) // 2
HY_FFN = 64
HY_INNER = 2
HY_FAST_PCT = 0.3
HY_SLOW_PCT = 1.5
HY_TARGET = 1e-2

ML_HEADS = 4
ML_DH = D_INNER // ML_HEADS
ML_DK = ML_DH // 2
ML_CHUNK = 128

LANES = 128
VMEM_LIMIT = 56 * 1024 * 1024

_NEG_INF = float("-inf")


def _params(*sem):
    return pltpu.CompilerParams(dimension_semantics=sem, vmem_limit_bytes=VMEM_LIMIT)


def _sigmoid(x):
    return 0.5 * jnp.tanh(0.5 * x) + 0.5


def _silu(x):
    return x * _sigmoid(x)


def _softplus(x):
    return jnp.maximum(x, 0.0) + jnp.log1p(jnp.exp(-jnp.abs(x)))


def _log_sigmoid(x):
    return -_softplus(-x)


def _dot(a, b):
    return jnp.dot(a, b, preferred_element_type=F32)


def _dot_nt(a, b):
    return lax.dot_general(a, b, (((1,), (1,)), ((), ())), preferred_element_type=F32)


def _dot_tn(a, b):
    return lax.dot_general(a, b, (((0,), (0,)), ((), ())), preferred_element_type=F32)


def _dot_f32(a, b):
    return jnp.dot(a, b, preferred_element_type=F32, precision=lax.Precision.HIGHEST)


def _split(v):
    hi = v.astype(BF16)
    lo = (v - hi.astype(F32)).astype(BF16)
    return hi, lo


def _tri_dot(t, v):
    hi, lo = _split(v)
    return _dot(t, hi) + _dot(t, lo)


def _dot_tri(v, t):
    hi, lo = _split(v)
    return _dot(hi, t) + _dot(lo, t)


def _tri_consts(q):
    ri = lax.broadcasted_iota(jnp.int32, (q, q), 0)
    ci = lax.broadcasted_iota(jnp.int32, (q, q), 1)
    lower = ri >= ci
    upper = ci >= ri
    t_lower = jnp.where(lower, 1.0, 0.0).astype(BF16)
    t_upper = jnp.where(upper, 1.0, 0.0).astype(BF16)
    return (lower, upper), (t_lower, t_upper)


def _shift_rows(x, k):
    if k == 0:
        return x
    n = x.shape[0]
    rows = lax.broadcasted_iota(jnp.int32, x.shape, 0)
    y = pltpu.roll(x, k % n, axis=0)
    if k > 0:
        return jnp.where(rows >= k, y, 0.0)
    return jnp.where(rows < n + k, y, 0.0)


def _dwconv(x, w, b):
    taps = w.shape[0]
    pad = (taps - 1) // 2
    acc = x * w[pad:pad + 1, :] + b
    for j in range(taps):
        if j != pad:
            acc = acc + _shift_rows(x, pad - j) * w[j:j + 1, :]
    return acc


def _rms(x):
    return x * lax.rsqrt(jnp.mean(x * x, axis=-1, keepdims=True) + EPS)


def _inproj_kernel(x_ref, g_ref, w_ref, o_ref, xn_ref):
    @pl.when(pl.program_id(1) == 0)
    def _():
        xn_ref[...] = (_rms(x_ref[...]) * g_ref[...]).astype(BF16)

    o_ref[...] = _dot(xn_ref[...], w_ref[...]).astype(o_ref.dtype)


def _inproj(x2d, g, w, out_dtype, tn):
    m, k = x2d.shape
    n = w.shape[1]
    tm = min(1024, m)
    return pl.pallas_call(
        _inproj_kernel,
        grid=(m // tm, n // tn),
        in_specs=[pl.BlockSpec((tm, k), lambda i, j: (i, 0)),
                  pl.BlockSpec((1, k), lambda i, j: (0, 0)),
                  pl.BlockSpec((k, tn), lambda i, j: (0, j))],
        out_specs=pl.BlockSpec((tm, tn), lambda i, j: (i, j)),
        out_shape=jax.ShapeDtypeStruct((m, n), out_dtype),
        scratch_shapes=[pltpu.VMEM((tm, k), BF16)],
        compiler_params=_params("parallel", "arbitrary"),
        name="inproj",
    )(x2d, g.reshape(1, k).astype(F32), w)


def _outproj_kernel(y_ref, w_ref, r_ref, g_ref, o_ref, *, final):
    acc = _dot(y_ref[...], w_ref[...]) + r_ref[...]
    if final:
        acc = _rms(acc) * g_ref[...]
    o_ref[...] = acc


def _outproj(y2d, w, res2d, final_g=None):
    m, k = y2d.shape
    n = w.shape[1]
    tm = min(512, m)
    final = final_g is not None
    g = (final_g if final else jnp.ones((n,), F32)).reshape(1, n).astype(F32)
    return pl.pallas_call(
        functools.partial(_outproj_kernel, final=final),
        grid=(m // tm,),
        in_specs=[pl.BlockSpec((tm, k), lambda i: (i, 0)),
                  pl.BlockSpec((k, n), lambda i: (0, 0)),
                  pl.BlockSpec((tm, n), lambda i: (i, 0)),
                  pl.BlockSpec((1, n), lambda i: (0, 0))],
        out_specs=pl.BlockSpec((tm, n), lambda i: (i, 0)),
        out_shape=jax.ShapeDtypeStruct((m, n), F32),
        compiler_params=_params("parallel"),
        name="outproj",
    )(y2d, w.astype(BF16), res2d, g)


def _pad_cols(w, n):
    return jnp.pad(w, ((0, 0), (0, n - w.shape[1])))


def _ssd_kernel(z_ref, x_ref, b_ref, c_ref, dtc_ref, dtr_ref,
                cwx_ref, cwb_ref, cwc_ref, cbx_ref, cbb_ref, cbc_ref,
                biasr_ref, biasc_ref, alogr_ref, alogc_ref, dskip_ref, gnorm_ref,
                o_ref, xs_ref, cs_ref, bt_ref, y_ref, s_ref, u_ref, e1_ref, dec_ref):
    q = SSD_CHUNK
    seq = x_ref.shape[0]
    nc = seq // q
    hpg = SSD_HPG
    assert hpg * SSD_HEAD_DIM == 2 * LANES
    (lower, upper), (t_lower, t_upper) = _tri_consts(q)
    lane_head = lax.shift_right_logical(
        lax.broadcasted_iota(jnp.int32, (1, SSD_GW), 1), int(math.log2(SSD_HEAD_DIM)))
    first_head = lax.broadcasted_iota(jnp.int32, (1, LANES), 1) < SSD_HEAD_DIM
    fwd_lane = lax.broadcasted_iota(jnp.int32, (1, 2 * hpg), 1) < hpg
    fwd_row = lax.broadcasted_iota(jnp.int32, (2 * hpg, 1), 0) < hpg

    xs_ref[...] = _silu(_dwconv(x_ref[...].astype(F32), cwx_ref[...], cbx_ref[...]))
    cs_ref[...] = _silu(_dwconv(c_ref[...].astype(F32), cwc_ref[...], cbc_ref[...])).astype(BF16)
    bs = _silu(_dwconv(b_ref[...].astype(F32), cwb_ref[...], cbb_ref[...]))
    for c in range(nc):
        bt_ref[c] = bs[c * q:(c + 1) * q, :].T.astype(BF16)

    bias_row = biasr_ref[...]
    bias_col = biasc_ref[...]
    a_row = -jnp.exp(alogr_ref[...])
    a_col = -jnp.exp(alogc_ref[...])
    dskip = dskip_ref[...]

    def rep(cols, j):
        return jnp.broadcast_to(cols[:, j:j + 1], (cols.shape[0], LANES))

    sel_dt = jnp.where(
        lax.broadcasted_iota(jnp.int32, (2 * hpg, 2 * SSD_GW), 0)
        == lax.shift_right_logical(lax.broadcasted_iota(jnp.int32, (2 * hpg, 2 * SSD_GW), 1),
                                   int(math.log2(SSD_HEAD_DIM))), 1.0, 0.0).astype(BF16)

    unroll = 2 if nc % 2 == 0 else 1

    def local_body(i, carry):
        chunks = [i * unroll + j for j in range(unroll)]
        rows = [pl.ds(pl.multiple_of(c * q, q), q) for c in chunks]
        us = range(unroll)
        dt_col = [_softplus(dtc_ref[r, :] + bias_row) for r in rows]
        dt_row = [_softplus(dtr_ref[c] + bias_col) for c in chunks]
        la_col = [dt * a_row for dt in dt_col]
        la_row = [dt * a_col for dt in dt_row]
        pre_col = [_tri_dot(t_lower, la) for la in la_col]
        pre_row = [_dot_tri(la, t_upper) for la in la_row]
        cbs = [_dot(cs_ref[rows[j], :], bt_ref[chunks[j]]) for j in us]
        dt_exp = [_dot(dt.astype(BF16), sel_dt) for dt in dt_col]
        ys, xqs, cum_exps = [], [], []
        for j in us:
            cum_col = jnp.where(fwd_lane, pre_col[j], pre_col[j][q - 1:q, :] - pre_col[j] + la_col[j])
            cum_row = jnp.where(fwd_row, pre_row[j], pre_row[j][:, q - 1:q] - pre_row[j] + la_row[j])
            reps = [rep(cum_col, r) for r in range(2 * hpg)]
            ms = []
            for r in range(hpg):
                rb = hpg + r
                mf = jnp.exp(jnp.where(lower, reps[r] - cum_row[r:r + 1, :], _NEG_INF)) * dt_row[j][r:r + 1, :]
                mb = jnp.exp(jnp.where(upper, reps[rb] - cum_row[rb:rb + 1, :], _NEG_INF)) * dt_row[j][rb:rb + 1, :]
                ms.append((cbs[j] * (mf + mb)).astype(BF16))
            xq = xs_ref[rows[j], :]
            xb = xq.astype(BF16)
            xbd = jnp.concatenate(
                [jnp.where(lane_head == r, xb, jnp.zeros_like(xb)) for r in range(hpg)], axis=0)
            ys.append(_dot(jnp.concatenate(ms, axis=1), xbd))
            xqs.append(xq)
            cum_exps.append([jnp.concatenate(
                [jnp.where(first_head, reps[lo], reps[lo + 1]),
                 jnp.where(first_head, reps[lo + 2], reps[lo + 3])], axis=1) for lo in (0, hpg)])
        for j in us:
            y_ref[rows[j], :] = ys[j] + xqs[j] * dskip
            for d in range(2):
                cum_exp = cum_exps[j][d]
                tot = cum_exp[q - 1:q, :] if d == 0 else cum_exp[0:1, :]
                w = dt_exp[j][:, d * SSD_GW:(d + 1) * SSD_GW] * jnp.exp(tot - cum_exp)
                u_ref[d, chunks[j]] = _dot(bt_ref[chunks[j]], (xqs[j] * w).astype(BF16))
                e1_ref[d, rows[j], :] = jnp.exp(cum_exp).astype(BF16)
                dec_ref[d, chunks[j]] = jnp.exp(tot)
        return carry

    lax.fori_loop(0, nc // unroll, local_body, 0)

    s_ref[...] = jnp.zeros_like(s_ref)

    def serial(i, carry):
        chunks = (i, nc - 1 - i)
        rows = [pl.ds(pl.multiple_of(c * q, q), q) for c in chunks]
        states = [s_ref[d] for d in range(2)]
        inter = [_dot(cs_ref[rows[d], :], states[d].astype(BF16)) for d in range(2)]
        for d in range(2):
            s_ref[d] = states[d] * dec_ref[d, chunks[d]] + u_ref[d, chunks[d]]
        for d in range(2):
            y_ref[rows[d], :] += inter[d] * e1_ref[d, rows[d], :].astype(F32)
        return carry

    lax.fori_loop(0, nc, serial, 0)

    y = y_ref[...] * _silu(z_ref[...].astype(F32))
    o_ref[...] = (_rms(y) * gnorm_ref[...]).astype(o_ref.dtype)


def _ssd_layer(h, norm, w_in, conv_w, conv_b, dt_bias, a_log, d_skip, gnorm, w_out, final_g):
    bsz, seq, _ = h.shape
    g, hpg, q = SSD_GROUPS, SSD_HPG, SSD_CHUNK
    nc = seq // q
    h2 = h.reshape(bsz * seq, D_MODEL)
    n_main = 2 * D_INNER + 2 * SSD_BC
    pm = _inproj(h2, norm, w_in[:, :n_main].astype(BF16), BF16, 1024).reshape(bsz, seq, n_main)
    gates = _inproj(h2, norm, _pad_cols(w_in[:, n_main:], LANES).astype(BF16), F32, LANES)
    dt = gates[:, :2 * SSD_HEADS].reshape(bsz, seq, 2, g, hpg)
    dt_col = dt.transpose(0, 3, 1, 2, 4).reshape(bsz, g, seq, 2 * hpg)
    dt_row = dt.reshape(bsz, nc, q, 2, g, hpg).transpose(0, 4, 1, 3, 5, 2).reshape(bsz, g, nc, 2 * hpg, q)

    def per_group(p):
        return p.astype(F32).reshape(2, g, hpg).transpose(1, 0, 2).reshape(g, 2 * hpg)

    bias, alog = per_group(dt_bias), per_group(a_log)
    dskip = jnp.repeat(d_skip.astype(F32).reshape(g, hpg), SSD_HEAD_DIM, axis=1).reshape(g, 1, SSD_GW)
    cw = conv_w.astype(F32)
    cb = conv_b.astype(F32).reshape(1, -1)
    taps = cw.shape[0]
    xo = D_INNER // SSD_GW
    bo = 2 * D_INNER // SSD_STATE
    co = bo + SSD_GROUPS
    cbo = D_INNER // SSD_STATE
    cco = cbo + SSD_GROUPS

    y = pl.pallas_call(
        _ssd_kernel,
        grid=(bsz, g),
        in_specs=[
            pl.BlockSpec((None, seq, SSD_GW), lambda b, j: (b, 0, j)),
            pl.BlockSpec((None, seq, SSD_GW), lambda b, j: (b, 0, xo + j)),
            pl.BlockSpec((None, seq, SSD_STATE), lambda b, j: (b, 0, bo + j)),
            pl.BlockSpec((None, seq, SSD_STATE), lambda b, j: (b, 0, co + j)),
            pl.BlockSpec((None, None, seq, 2 * hpg), lambda b, j: (b, j, 0, 0)),
            pl.BlockSpec((None, None, nc, 2 * hpg, q), lambda b, j: (b, j, 0, 0, 0)),
            pl.BlockSpec((taps, SSD_GW), lambda b, j: (0, j)),
            pl.BlockSpec((taps, SSD_STATE), lambda b, j: (0, cbo + j)),
            pl.BlockSpec((taps, SSD_STATE), lambda b, j: (0, cco + j)),
            pl.BlockSpec((1, SSD_GW), lambda b, j: (0, j)),
            pl.BlockSpec((1, SSD_STATE), lambda b, j: (0, cbo + j)),
            pl.BlockSpec((1, SSD_STATE), lambda b, j: (0, cco + j)),
            pl.BlockSpec((None, 1, 2 * hpg), lambda b, j: (j, 0, 0)),
            pl.BlockSpec((None, 2 * hpg, 1), lambda b, j: (j, 0, 0)),
            pl.BlockSpec((None, 1, 2 * hpg), lambda b, j: (j, 0, 0)),
            pl.BlockSpec((None, 2 * hpg, 1), lambda b, j: (j, 0, 0)),
            pl.BlockSpec((None, 1, SSD_GW), lambda b, j: (j, 0, 0)),
            pl.BlockSpec((None, 1, SSD_GW), lambda b, j: (j, 0, 0)),
        ],
        out_specs=pl.BlockSpec((None, seq, SSD_GW), lambda b, j: (b, 0, j)),
        out_shape=jax.ShapeDtypeStruct((bsz, seq, D_INNER), BF16),
        scratch_shapes=[pltpu.VMEM((seq, SSD_GW), F32),
                        pltpu.VMEM((seq, SSD_STATE), BF16),
                        pltpu.VMEM((nc, SSD_STATE, q), BF16),
                        pltpu.VMEM((seq, SSD_GW), F32),
                        pltpu.VMEM((2, SSD_STATE, SSD_GW), F32),
                        pltpu.VMEM((2, nc, SSD_STATE, SSD_GW), F32),
                        pltpu.VMEM((2, seq, SSD_GW), BF16),
                        pltpu.VMEM((2, nc, 1, SSD_GW), F32)],
        compiler_params=_params("parallel", "parallel"),
        name="ssd_mixer",
    )(pm, pm, pm, pm, dt_col, dt_row, cw, cw, cw, cb, cb, cb,
      bias.reshape(g, 1, 2 * hpg), bias.reshape(g, 2 * hpg, 1),
      alog.reshape(g, 1, 2 * hpg), alog.reshape(g, 2 * hpg, 1),
      dskip, gnorm.astype(F32).reshape(g, 1, SSD_GW))
    out = _outproj(y.reshape(bsz * seq, D_INNER), w_out, h2, final_g)
    return out.reshape(bsz, seq, D_MODEL)


def _gla_kernel(q_ref, k_ref, v_ref, z_ref, gl_ref, wg_ref, bg_ref, onorm_ref,
                o_ref, lg_ref, acc_ref, s_ref, qg_ref, u_ref, dec_ref):
    q = GLA_CHUNK
    seq = q_ref.shape[0]
    nc = seq // q
    masks, tris = _tri_consts(q)
    gl = gl_ref[...]
    for d in range(2):
        lg_ref[d] = _log_sigmoid(_dot_tri(gl, wg_ref[d]) + bg_ref[d]) * (1.0 / GLA_NORMALIZER)

    unroll = 2 if nc % 2 == 0 else 1

    def local_body(i, carry):
        chunks = [i * unroll + j for j in range(unroll)]
        rows = [pl.ds(pl.multiple_of(c * q, q), q) for c in chunks]
        pairs = [(j, d) for j in range(unroll) for d in range(2)]
        cums = {(j, d): _tri_dot(tris[d], lg_ref[d, rows[j], :]) for j, d in pairs}
        qgs, kgs, kds = {}, {}, {}
        for j, d in pairs:
            cum = cums[j, d]
            qc = q_ref[rows[j], :].astype(F32) * (GLA_DK ** -0.5)
            kc = k_ref[rows[j], :].astype(F32)
            tot = cum[q - 1:q, :] if d == 0 else cum[0:1, :]
            qgs[j, d] = (qc * jnp.exp(cum)).astype(BF16)
            kgs[j, d] = (kc * jnp.exp(-cum)).astype(BF16)
            kds[j, d] = (kc * jnp.exp(tot - cum)).astype(BF16)
            qg_ref[d, rows[j], :] = qgs[j, d]
            dec_ref[d, chunks[j]] = jnp.exp(tot)
        atts = {p: _dot_nt(qgs[p], kgs[p]) for p in pairs}
        for j, d in pairs:
            u_ref[d, chunks[j]] = _dot_tn(v_ref[rows[j], :], kds[j, d]).astype(BF16)
        atts = {(j, d): jnp.where(masks[d], atts[j, d], 0.0).astype(BF16) for j, d in pairs}
        outs = {(j, d): _dot(atts[j, d], v_ref[rows[j], :]) for j, d in pairs}
        for j in range(unroll):
            acc_ref[rows[j], :] = outs[j, 0] + outs[j, 1]
        return carry

    lax.fori_loop(0, nc // unroll, local_body, 0)

    s_ref[...] = jnp.zeros_like(s_ref)

    def serial(i, carry):
        chunks = (i, nc - 1 - i)
        rows = [pl.ds(pl.multiple_of(c * q, q), q) for c in chunks]
        states = [s_ref[d] for d in range(2)]
        inter = [_dot_nt(qg_ref[d, rows[d], :], states[d].astype(BF16)) for d in range(2)]
        for d in range(2):
            s_ref[d] = states[d] * dec_ref[d, chunks[d]] + u_ref[d, chunks[d]].astype(F32)
        for d in range(2):
            acc_ref[rows[d], :] += inter[d]
        return carry

    lax.fori_loop(0, nc, serial, 0)

    out = _rms(acc_ref[...]) * onorm_ref[...]
    o_ref[...] = (out * _silu(z_ref[...].astype(F32))).astype(o_ref.dtype)


def _gla_layer(h, norm, w_in, w_gate, b_gate, onorm, w_out, final_g):
    bsz, seq, _ = h.shape
    nh = GLA_HEADS
    h2 = h.reshape(bsz * seq, D_MODEL)
    n_main = 2 * GLA_QK + 2 * D_INNER
    pm = _inproj(h2, norm, w_in[:, :n_main].astype(BF16), BF16, 1024).reshape(bsz, seq, n_main)
    gates = _inproj(h2, norm, _pad_cols(w_in[:, n_main:], LANES).astype(BF16), F32, LANES)
    gates = gates.reshape(bsz, seq, LANES)
    wg = jnp.zeros((2, LANES, GLA_QK), F32)
    for d in range(2):
        wg = wg.at[d, d * GLA_RANK:(d + 1) * GLA_RANK, :].set(w_gate[d].astype(F32))
    vo = 2 * GLA_QK // GLA_DV
    zo = vo + nh
    y = pl.pallas_call(
        _gla_kernel,
        grid=(bsz, nh),
        in_specs=[
            pl.BlockSpec((None, seq, GLA_DK), lambda b, j: (b, 0, j)),
            pl.BlockSpec((None, seq, GLA_DK), lambda b, j: (b, 0, nh + j)),
            pl.BlockSpec((None, seq, GLA_DV), lambda b, j: (b, 0, vo + j)),
            pl.BlockSpec((None, seq, GLA_DV), lambda b, j: (b, 0, zo + j)),
            pl.BlockSpec((None, seq, LANES), lambda b, j: (b, 0, 0)),
            pl.BlockSpec((2, LANES, GLA_DK), lambda b, j: (0, 0, j)),
            pl.BlockSpec((2, 1, GLA_DK), lambda b, j: (0, 0, j)),
            pl.BlockSpec((1, GLA_DV), lambda b, j: (0, 0)),
        ],
        out_specs=pl.BlockSpec((None, seq, GLA_DV), lambda b, j: (b, 0, j)),
        out_shape=jax.ShapeDtypeStruct((bsz, seq, D_INNER), BF16),
        scratch_shapes=[pltpu.VMEM((2, seq, GLA_DK), F32),
                        pltpu.VMEM((seq, GLA_DV), F32),
                        pltpu.VMEM((2, GLA_DV, GLA_DK), F32),
                        pltpu.VMEM((2, seq, GLA_DK), BF16),
                        pltpu.VMEM((2, seq // GLA_CHUNK, GLA_DV, GLA_DK), BF16),
                        pltpu.VMEM((2, seq // GLA_CHUNK, 1, GLA_DK), F32)],
        compiler_params=_params("parallel", "parallel"),
        name="gla_mixer",
    )(pm, pm, pm, pm, gates, wg.astype(BF16), b_gate.astype(F32).reshape(2, 1, GLA_QK),
      onorm.astype(F32).reshape(1, GLA_DV))
    out = _outproj(y.reshape(bsz * seq, D_INNER), w_out, h2, final_g)
    return out.reshape(bsz, seq, D_MODEL)


def _hy_mlp_kernel(f_ref, w0_ref, b0_ref, w1_ref, b1_ref, w2_ref, b2_ref, fr_ref, o_ref):
    fr = fr_ref[...]
    hid = jnp.sin(fr[0:1, :] * (_dot_f32(f_ref[...], w0_ref[...]) + b0_ref[...]))
    hid = jnp.sin(fr[1:2, :] * (_dot_f32(hid, w1_ref[...]) + b1_ref[...]))
    hid = jnp.sin(fr[2:3, :] * (_dot_f32(hid, w2_ref[...]) + b2_ref[...]))
    o_ref[...] = hid


def _hy_spec_kernel(hid_ref, wf_ref, wb_ref, t_ref, dl_ref, cos_ref, sin_ref,
                    kre_ref, ks_ref, a_ref, b_ref):
    seq = hid_ref.shape[0]

    @pl.when(pl.program_id(2) == 0)
    def _():
        env = jnp.exp(-t_ref[...] * dl_ref[...])
        hf = _dot_f32(hid_ref[...], wf_ref[...]) * env
        hb = _dot_f32(hid_ref[...], wb_ref[...]) * env
        rows = lax.broadcasted_iota(jnp.int32, hb.shape, 0)
        hb = jnp.where(rows == 0, 0.0, hb)
        a_ref[...] = (hf + hb).astype(BF16)
        b_ref[...] = (hf - hb).astype(BF16)

    scale = 1.0 / seq
    kre_ref[...] = (_dot(cos_ref[...], a_ref[...]) * scale).astype(kre_ref.dtype)
    ks_ref[...] = (_dot(sin_ref[...], b_ref[...]) * scale).astype(ks_ref.dtype)


def _hy_conv_kernel(v_ref, x1_ref, x2_ref, z_ref, cwv_ref, cw1_ref, cw2_ref,
                    cbv_ref, cb1_ref, cb2_ref, kre_ref, ks_ref, d_ref, w_ref, o_ref,
                    y_ref, yb_ref, zf_ref, g_ref, *, fb):
    seq = v_ref.shape[0]
    nblk = seq // fb
    y_ref[...] = _dwconv(v_ref[...].astype(F32), cwv_ref[...], cbv_ref[...])
    gates = ((x1_ref, cw1_ref, cb1_ref), (x2_ref, cw2_ref, cb2_ref))
    for o, (x_ref, cw_ref, cb_ref) in enumerate(gates):
        yb_ref[...] = y_ref[...].astype(BF16)
        g_ref[...] = _dwconv(x_ref[...].astype(F32), cw_ref[...], cb_ref[...])

        def forward(k, carry, o=o):
            f0 = pl.multiple_of(k * fb, fb)
            yb = yb_ref[...]
            yre = _dot(w_ref[pl.ds(f0, fb), :], yb)
            yim = _dot(w_ref[pl.ds(seq + f0, fb), :], yb)
            kre = kre_ref[o, pl.ds(f0, fb), :].astype(F32)
            ks = ks_ref[o, pl.ds(f0, fb), :].astype(F32)
            zf_ref[pl.ds(f0, fb), :] = (yre * kre - yim * ks).astype(BF16)
            zf_ref[pl.ds(seq + f0, fb), :] = (yim * kre + yre * ks).astype(BF16)
            return carry

        lax.fori_loop(0, nblk, forward, 0)

        def inverse(k, carry, o=o):
            t0 = pl.multiple_of(k * fb, fb)
            conv = (_dot(w_ref[pl.ds(t0, fb), :], zf_ref[0:seq, :])
                    + _dot(w_ref[pl.ds(seq + t0, fb), :], zf_ref[seq:2 * seq, :]))
            rows = pl.ds(t0, fb)
            y_ref[rows, :] = g_ref[rows, :] * (conv + y_ref[rows, :] * d_ref[o:o + 1, :])
            return carry

        lax.fori_loop(0, nblk, inverse, 0)
    o_ref[...] = (y_ref[...] * _silu(z_ref[...].astype(F32))).astype(o_ref.dtype)


def _hyena_layer(h, norm, w_in, conv_w, conv_b, ffn_w_in, ffn_b_in, ffn_w_hid, ffn_b_hid,
                 ffn_freq, ffn_w_out, d_bias, w_out, final_g):
    bsz, seq, _ = h.shape
    c = D_INNER
    h2 = h.reshape(bsz * seq, D_MODEL)
    pm = _inproj(h2, norm, w_in.astype(BF16), BF16, 1024).reshape(bsz, seq, 4 * c)

    t = jnp.linspace(0.0, 1.0, seq, dtype=F32)[:, None]
    pos = jnp.arange(seq, dtype=F32)[:, None]
    bands = jnp.linspace(1e-4, HY_BANDS - 1, HY_BANDS, dtype=F32)[None]
    ang = (2.0 * math.pi / seq) * pos * bands
    feats = jnp.concatenate([t, jnp.cos(ang), -jnp.sin(ang)], axis=-1)
    feats = _pad_cols(feats, LANES)
    w0 = jnp.pad(ffn_w_in.astype(F32), ((0, LANES - HY_EMB), (0, 0)))
    hid = pl.pallas_call(
        _hy_mlp_kernel,
        out_shape=jax.ShapeDtypeStruct((seq, HY_FFN), F32),
        name="hyena_filter_mlp",
    )(feats, w0, ffn_b_in.astype(F32).reshape(1, HY_FFN),
      ffn_w_hid[0].astype(F32), ffn_b_hid[0].astype(F32).reshape(1, HY_FFN),
      ffn_w_hid[1].astype(F32), ffn_b_hid[1].astype(F32).reshape(1, HY_FFN),
      ffn_freq.astype(F32))

    max_decay = math.log(HY_TARGET) / HY_FAST_PCT
    min_decay = math.log(HY_TARGET) / HY_SLOW_PCT
    deltas = jnp.abs(jnp.linspace(min_decay, max_decay, c, dtype=F32)).reshape(1, c)

    fi = jnp.arange(seq, dtype=jnp.int32)
    n4 = 8 * seq

    def table(num, fn):
        return fn((2.0 * math.pi / n4) * (num % n4).astype(F32)).astype(BF16)

    num1 = (2 * fi[:, None] + 1) * (2 * fi[None, :])
    num2 = (2 * fi[:, None] + 1) * (2 * fi[None, :] + 1)
    cos1, sin1 = table(num1, jnp.cos), table(num1, jnp.sin)
    w2 = jnp.concatenate([table(num2, jnp.cos), table(num2, jnp.sin)], axis=0)

    tc = min(512, c)
    fb = min(512, seq)
    nct = c // tc
    wo = ffn_w_out.astype(F32)
    kre, ks = pl.pallas_call(
        _hy_spec_kernel,
        grid=(HY_ORDER, nct, seq // fb),
        in_specs=[
            pl.BlockSpec((seq, HY_FFN), lambda o, j, k: (0, 0)),
            pl.BlockSpec((HY_FFN, tc), lambda o, j, k: (0, o * nct + j)),
            pl.BlockSpec((HY_FFN, tc), lambda o, j, k: (0, (HY_ORDER + o) * nct + j)),
            pl.BlockSpec((seq, 1), lambda o, j, k: (0, 0)),
            pl.BlockSpec((1, tc), lambda o, j, k: (0, j)),
            pl.BlockSpec((fb, seq), lambda o, j, k: (k, 0)),
            pl.BlockSpec((fb, seq), lambda o, j, k: (k, 0)),
        ],
        out_specs=[pl.BlockSpec((None, fb, tc), lambda o, j, k: (o, k, j)),
                   pl.BlockSpec((None, fb, tc), lambda o, j, k: (o, k, j))],
        out_shape=[jax.ShapeDtypeStruct((HY_ORDER, seq, c), BF16)] * 2,
        scratch_shapes=[pltpu.VMEM((seq, tc), BF16), pltpu.VMEM((seq, tc), BF16)],
        compiler_params=_params("parallel", "parallel", "arbitrary"),
        name="hyena_filter_spectrum",
    )(hid, wo, wo, t, deltas, cos1, sin1)

    tcc = min(256, c)
    ncc = c // tcc
    cw = conv_w.astype(F32)
    cb = conv_b.astype(F32).reshape(1, -1)
    taps = cw.shape[0]
    y = pl.pallas_call(
        functools.partial(_hy_conv_kernel, fb=fb),
        grid=(ncc, bsz),
        in_specs=[
            pl.BlockSpec((None, seq, tcc), lambda j, b: (b, 0, j)),
            pl.BlockSpec((None, seq, tcc), lambda j, b: (b, 0, ncc + j)),
            pl.BlockSpec((None, seq, tcc), lambda j, b: (b, 0, 2 * ncc + j)),
            pl.BlockSpec((None, seq, tcc), lambda j, b: (b, 0, 3 * ncc + j)),
            pl.BlockSpec((taps, tcc), lambda j, b: (0, j)),
            pl.BlockSpec((taps, tcc), lambda j, b: (0, ncc + j)),
            pl.BlockSpec((taps, tcc), lambda j, b: (0, 2 * ncc + j)),
            pl.BlockSpec((1, tcc), lambda j, b: (0, j)),
            pl.BlockSpec((1, tcc), lambda j, b: (0, ncc + j)),
            pl.BlockSpec((1, tcc), lambda j, b: (0, 2 * ncc + j)),
            pl.BlockSpec((HY_ORDER, seq, tcc), lambda j, b: (0, 0, j), pipeline_mode=pl.Buffered(1)),
            pl.BlockSpec((HY_ORDER, seq, tcc), lambda j, b: (0, 0, j), pipeline_mode=pl.Buffered(1)),
            pl.BlockSpec((HY_ORDER, tcc), lambda j, b: (0, j)),
            pl.BlockSpec((2 * seq, seq), lambda j, b: (0, 0), pipeline_mode=pl.Buffered(1)),
        ],
        out_specs=pl.BlockSpec((None, seq, tcc), lambda j, b: (b, 0, j)),
        out_shape=jax.ShapeDtypeStruct((bsz, seq, c), BF16),
        scratch_shapes=[pltpu.VMEM((seq, tcc), F32),
                        pltpu.VMEM((seq, tcc), BF16),
                        pltpu.VMEM((2 * seq, tcc), BF16),
                        pltpu.VMEM((seq, tcc), F32)],
        compiler_params=_params("parallel", "parallel"),
        name="hyena_mixer",
    )(pm, pm, pm, pm, cw, cw, cw, cb, cb, cb, kre, ks, d_bias.astype(F32), w2)
    out = _outproj(y.reshape(bsz * seq, c), w_out, h2, final_g)
    return out.reshape(bsz, seq, D_MODEL)


def _mlstm_kernel(xm_ref, z_ref, og_ref, gc_ref, gr_ref, cw_ref, cb_ref, wq_ref, wk_ref, wv_ref,
                  gbr_ref, gbc_ref, skip_ref, onorm_ref,
                  o_ref, q_ref, k_ref, v_ref, ch_ref, acc_ref, c_ref, nl_ref, st_ref, kt_ref, str_ref, cmat_ref, inc_ref):
    q = ML_CHUNK
    seq = xm_ref.shape[0]
    nc = seq // q
    masks, tris = _tri_consts(q)

    for j in range(ML_DH // LANES):
        sl = slice(j * LANES, (j + 1) * LANES)
        ch_ref[:, sl] = _silu(_dwconv(xm_ref[:, sl].astype(F32), cw_ref[:, sl], cb_ref[:, sl])).astype(BF16)
    ch = ch_ref[...]
    q_ref[...] = _dot(ch, wq_ref[...]).astype(BF16)
    kf = _dot(ch, wk_ref[...]) * (ML_DK ** -0.5)
    k_ref[...] = kf.astype(BF16)
    for c in range(nc):
        kt_ref[c] = kf[c * q:(c + 1) * q, :].T.astype(BF16)
    v_ref[...] =_dot(xm_ref[...], wv_ref[...]).astype(BF16)

    gb_row = gbr_ref[...]
    gb_col = gbc_ref[...]

    (t_lower, t_upper) = tris
    stat_lane = lax.broadcasted_iota(jnp.int32, (1, 8), 1)

    unroll = 2 if nc % 2 == 0 else 1

    def local_body(i, carry):
        chunks = [i * unroll + j for j in range(unroll)]
        rows = [pl.ds(pl.multiple_of(c * q, q), q) for c in chunks]
        pairs = [(j, d) for j in range(unroll) for d in range(2)]
        g_col = [gc_ref[r, :] + gb_row for r in rows]
        g_row = [gr_ref[c] + gb_col for c in chunks]
        lf_col = [_log_sigmoid(g) for g in g_col]
        lf_row = [_log_sigmoid(g) for g in g_row]
        tri_c, tri_r = (t_lower, t_upper), (t_upper, t_lower)
        b_cols = {(j, d): _tri_dot(tri_c[d], lf_col[j]) for j, d in pairs}
        b_rows = {(j, d): _dot_tri(lf_row[j], tri_r[d]) for j, d in pairs}
        qk = [_dot_nt(q_ref[r, :], k_ref[r, :]) for r in rows]
        ss = {}
        for j, d in pairs:
            i_col = g_col[j][:, 2 * d:2 * d + 1]
            i_row = g_row[j][2 * d:2 * d + 1, :]
            b_col = b_cols[j, d][:, 2 * d + 1:2 * d + 2]
            b_row = b_rows[j, d][2 * d + 1:2 * d + 2, :]
            logd = jnp.where(masks[d], b_col - b_row + i_row, _NEG_INF)
            m_loc = jnp.max(logd, axis=1, keepdims=True)
            s = qk[j] * jnp.exp(logd - m_loc)
            ss[j, d] = s.astype(BF16)
            den_loc = jnp.sum(s, axis=1, keepdims=True)
            tot = b_col[q - 1:q, :] if d == 0 else b_col[0:1, :]
            lw_col = tot - b_col + i_col
            lw_max = jnp.max(lw_col, axis=0, keepdims=True)
            stats = (m_loc, den_loc, b_col, lw_col - lw_max, jnp.broadcast_to(lw_max, (q, 1)))
            packed = stats[-1]
            for lane in range(len(stats) - 2, -1, -1):
                packed = jnp.where(stat_lane == lane, stats[lane], packed)
            st_ref[d, rows[j], :] = packed
            str_ref[d, chunks[j]] = jnp.broadcast_to(tot - b_row + i_row - lw_max, (8, q))
        for j, d in pairs:
            nl_ref[d, rows[j], :] = _dot(ss[j, d], v_ref[rows[j], :]).astype(BF16)
        return carry

    lax.fori_loop(0, nc // unroll, local_body, 0)

    c_ref[...] = jnp.zeros_like(c_ref)
    cmat_ref[...] = jnp.zeros_like(cmat_ref)
    acc_ref[...] = jnp.zeros_like(acc_ref)

    def serial(i, carry):
        chunks = (i, nc - 1 - i)
        rows = [pl.ds(pl.multiple_of(c * q, q), q) for c in chunks]
        w_rows = [jnp.exp(str_ref[d, chunks[d]]).astype(BF16) for d in range(2)]
        for d in range(2):
            inc_ref[d] = _dot(kt_ref[chunks[d]] * w_rows[d][0:1, :], v_ref[rows[d], :])
        ksums = [_dot(w_rows[d], k_ref[rows[d], :])[0:1, :] for d in range(2)]
        new, decs, gains = [], [], []
        for d in range(2):
            n_row, m = carry[d]
            st = st_ref[d, rows[d], :]
            m_loc, den_loc, b_col, lw_max = st[:, 0:1], st[:, 1:2], st[:, 2:3], st[0:1, 4:5]
            inter = b_col + m
            m_row = jnp.maximum(m_loc, inter)
            a_loc = jnp.exp(m_loc - m_row)
            a_int = jnp.exp(inter - m_row)
            tot = b_col[q - 1:q, :] if d == 0 else b_col[0:1, :]
            m_new = jnp.maximum(tot + m, lw_max)
            dec = jnp.exp(tot + m - m_new)
            gain = jnp.exp(lw_max - m_new)
            qc = q_ref[rows[d], :]
            den = den_loc * a_loc + jnp.sum(qc.astype(F32) * n_row, axis=1, keepdims=True) * a_int
            scale = 1.0 / jnp.maximum(jnp.abs(den), jnp.exp(-m_row))
            acc_ref[rows[d], :] += (nl_ref[d, rows[d], :].astype(F32) * (a_loc * scale)
                                    + _dot(qc, cmat_ref[d]) * (a_int * scale))
            decs.append(dec)
            gains.append(gain)
            new.append((n_row * dec + ksums[d] * gain, m_new))

        blk = 32
        for d in range(2):
            for r in range(0, ML_DK, blk):
                cnew = c_ref[d, r:r + blk, :] * decs[d] + inc_ref[d, r:r + blk, :] * gains[d]
                c_ref[d, r:r + blk, :] = cnew
                cmat_ref[d, r:r + blk, :] = cnew.astype(BF16)
        return tuple(new)

    init = (jnp.zeros((1, ML_DK), F32), jnp.full((1, 1), _NEG_INF, F32))
    lax.fori_loop(0, nc, serial, (init, init))

    hh = _sigmoid(og_ref[...].astype(F32)) * acc_ref[...]
    hh = _rms(hh) * onorm_ref[...] + skip_ref[...] * ch_ref[...].astype(F32)
    o_ref[...] = (hh * _silu(z_ref[...].astype(F32))).astype(o_ref.dtype)


def _mlstm_layer(h, norm, w_in, conv_w, conv_b, w_q, w_k, w_v, gate_b, skip, onorm, w_out, final_g):
    bsz, seq, _ = h.shape
    nh, q = ML_HEADS, ML_CHUNK
    nc = seq // q
    h2 = h.reshape(bsz * seq, D_MODEL)
    n_main = 3 * D_INNER
    pm = _inproj(h2, norm, w_in[:, :n_main].astype(BF16), BF16, 1024).reshape(bsz, seq, n_main)
    gates = _inproj(h2, norm, _pad_cols(w_in[:, n_main:], LANES).astype(BF16), F32, LANES)
    gt = gates[:, :4 * nh].reshape(bsz, seq, 2, 2, nh)
    g_col = gt.transpose(0, 4, 1, 2, 3).reshape(bsz, nh, seq, 4)
    g_row = gt.reshape(bsz, nc, q, 2, 2, nh).transpose(0, 5, 1, 3, 4, 2).reshape(bsz, nh, nc, 4, q)
    g_row = jnp.pad(g_row, ((0, 0), (0, 0), (0, 0), (0, 4), (0, 0)))
    gb =gate_b.astype(F32).transpose(2, 0, 1).reshape(nh, 4)
    taps = conv_w.shape[0]
    y = pl.pallas_call(
        _mlstm_kernel,
        grid=(bsz, nh),
        in_specs=[
            pl.BlockSpec((None, seq, ML_DH), lambda b, j: (b, 0, j)),
            pl.BlockSpec((None, seq, ML_DH), lambda b, j: (b, 0, nh + j)),
            pl.BlockSpec((None, seq, ML_DH), lambda b, j: (b, 0, 2 * nh + j)),
            pl.BlockSpec((None, None, seq, 4), lambda b, j: (b, j, 0, 0)),
            pl.BlockSpec((None, None, nc, 8, q), lambda b, j: (b, j, 0, 0, 0)),
            pl.BlockSpec((taps, ML_DH), lambda b, j: (0, j)),
            pl.BlockSpec((1, ML_DH), lambda b, j: (0, j)),
            pl.BlockSpec((None, ML_DH, ML_DK), lambda b, j: (j, 0, 0)),
            pl.BlockSpec((None, ML_DH, ML_DK), lambda b, j: (j, 0, 0)),
            pl.BlockSpec((None, ML_DH, ML_DH), lambda b, j: (j, 0, 0)),
            pl.BlockSpec((None, 1, 4), lambda b, j: (j, 0, 0)),
            pl.BlockSpec((None, 8, 1), lambda b, j: (j, 0, 0)),
            pl.BlockSpec((None, 1, ML_DH), lambda b, j: (j, 0, 0)),
            pl.BlockSpec((1, ML_DH), lambda b, j: (0, 0)),
        ],
        out_specs=pl.BlockSpec((None, seq, ML_DH), lambda b, j: (b, 0, j)),
        out_shape=jax.ShapeDtypeStruct((bsz, seq, D_INNER), BF16),
        scratch_shapes=[pltpu.VMEM((seq, ML_DK), BF16),
                        pltpu.VMEM((seq, ML_DK), BF16),
                        pltpu.VMEM((seq, ML_DH), BF16),
                        pltpu.VMEM((seq, ML_DH), BF16),
                        pltpu.VMEM((seq, ML_DH), F32),
                        pltpu.VMEM((2, ML_DK, ML_DH), F32),
                        pltpu.VMEM((2, seq, ML_DH), BF16),
                        pltpu.VMEM((2, seq, 8), F32),
                        pltpu.VMEM((nc, ML_DK, q), BF16),
                        pltpu.VMEM((2, nc, 8, q), F32),
                        pltpu.VMEM((2, ML_DK, ML_DH), BF16),
                        pltpu.VMEM((2, ML_DK, ML_DH), F32)],
        compiler_params=_params("parallel", "parallel"),
        name="mlstm_mixer",
    )(pm, pm, pm, g_col, g_row, conv_w.astype(F32), conv_b.astype(F32).reshape(1, -1),
      w_q.astype(BF16), w_k.astype(BF16), w_v.astype(BF16),
      gb.reshape(nh, 1, 4), jnp.pad(gb, ((0, 0), (0, 4))).reshape(nh, 8, 1),
      skip.astype(F32).reshape(nh, 1, ML_DH), onorm.astype(F32).reshape(1, ML_DH))
    out = _outproj(y.reshape(bsz * seq, D_INNER), w_out, h2, final_g)
    return out.reshape(bsz, seq, D_MODEL)


def kernel(x, ssd_norm, ssd_w_in, ssd_conv_w, ssd_conv_b, ssd_dt_bias, ssd_a_log, ssd_d, ssd_gnorm, ssd_w_out, gla_norm, gla_w_in, gla_w_gate, gla_b_gate, gla_onorm, gla_w_out, hy_norm, hy_w_in, hy_conv_w, hy_conv_b, hy_ffn_w_in, hy_ffn_b_in, hy_ffn_w_hid, hy_ffn_b_hid, hy_ffn_freq, hy_ffn_w_out, hy_d, hy_w_out, ml_norm, ml_w_in, ml_conv_w, ml_conv_b, ml_w_q, ml_w_k, ml_w_v, ml_gate_b, ml_skip, ml_onorm, ml_w_out, final_norm):
    depth = ssd_norm.shape[0] + gla_norm.shape[0] + hy_norm.shape[0] + ml_norm.shape[0]
    h = x
    for i in range(depth):
        kind, j = i % 4, i // 4
        fg = final_norm if i == depth - 1 else None
        if kind == 0:
            h = _ssd_layer(h, ssd_norm[j], ssd_w_in[j], ssd_conv_w[j], ssd_conv_b[j], ssd_dt_bias[j],
                           ssd_a_log[j], ssd_d[j], ssd_gnorm[j], ssd_w_out[j], fg)
        elif kind == 1:
            h = _gla_layer(h, gla_norm[j], gla_w_in[j], gla_w_gate[j], gla_b_gate[j], gla_onorm[j],
                           gla_w_out[j], fg)
        elif kind == 2:
            h = _hyena_layer(h, hy_norm[j], hy_w_in[j], hy_conv_w[j], hy_conv_b[j], hy_ffn_w_in[j],
                             hy_ffn_b_in[j], hy_ffn_w_hid[j], hy_ffn_b_hid[j], hy_ffn_freq[j],
                             hy_ffn_w_out[j], hy_d[j], hy_w_out[j], fg)
        else:
            h = _mlstm_layer(h, ml_norm[j], ml_w_in[j], ml_conv_w[j], ml_conv_b[j], ml_w_q[j], ml_w_k[j],
                             ml_w_v[j], ml_gate_b[j], ml_skip[j], ml_onorm[j], ml_w_out[j], fg)
    return h
```

```python
import functools
import math

import jax
import jax.numpy as jnp
from jax import lax
from jax.experimental import pallas as pl
from jax.experimental.pallas import tpu as pltpu

F32 = jnp.float32
BF16 = jnp.bfloat16

D_MODEL = 1024
D_INNER = 2 * D_MODEL
EPS = 1e-6

SSD_HEAD_DIM = 64
SSD_HEADS = D_INNER // SSD_HEAD_DIM
SSD_GROUPS = 8
SSD_HPG = SSD_HEADS // SSD_GROUPS
SSD_STATE = 128
SSD_CHUNK = 128
SSD_GW = SSD_HPG * SSD_HEAD_DIM
SSD_BC = SSD_GROUPS * SSD_STATE

GLA_HEADS = 4
GLA_DK = D_MODEL // 2 // GLA_HEADS
GLA_DV = D_INNER // GLA_HEADS
GLA_RANK = 16
GLA_NORMALIZER = 16.0
GLA_CHUNK = 64
GLA_QK = GLA_HEADS * GLA_DK

HY_ORDER = 2
HY_EMB = 33
HY_BANDS = (HY_EMB - 1) // 2
HY_FFN = 64
HY_INNER = 2
HY_FAST_PCT = 0.3
HY_SLOW_PCT = 1.5
HY_TARGET = 1e-2

ML_HEADS = 4
ML_DH = D_INNER // ML_HEADS
ML_DK = ML_DH // 2
ML_CHUNK = 128

LANES = 128
VMEM_LIMIT = 56 * 1024 * 1024

_NEG_INF = float("-inf")


def _params(*sem):
    return pltpu.CompilerParams(dimension_semantics=sem, vmem_limit_bytes=VMEM_LIMIT)


def _sigmoid(x):
    return 0.5 * jnp.tanh(0.5 * x) + 0.5


def _silu(x):
    half = 0.5 * x
    return half + half * jnp.tanh(half)


def _softplus(x):
    return jnp.maximum(x, 0.0) + jnp.log1p(jnp.exp(-jnp.abs(x)))


def _log_sigmoid(x):
    return -_softplus(-x)


def _dot(a, b):
    return jnp.dot(a, b, preferred_element_type=F32)


def _dot_nt(a, b):
    return lax.dot_general(a, b, (((1,), (1,)), ((), ())), preferred_element_type=F32)


def _dot_tn(a, b):
    return lax.dot_general(a, b, (((0,), (0,)), ((), ())), preferred_element_type=F32)


def _dot_f32(a, b):
    return jnp.dot(a, b, preferred_element_type=F32, precision=lax.Precision.HIGHEST)


def _split(v):
    hi = v.astype(BF16)
    lo = (v - hi.astype(F32)).astype(BF16)
    return hi, lo


def _tri_dot(t, v):
    hi, lo = _split(v)
    return _dot(t, hi) + _dot(t, lo)


def _dot_tri(v, t):
    hi, lo = _split(v)
    return _dot(hi, t) + _dot(lo, t)


def _tri_consts(q):
    ri = lax.broadcasted_iota(jnp.int32, (q, q), 0)
    ci = lax.broadcasted_iota(jnp.int32, (q, q), 1)
    lower = ri >= ci
    upper = ci >= ri
    t_lower = jnp.where(lower, 1.0, 0.0).astype(BF16)
    t_upper = jnp.where(upper, 1.0, 0.0).astype(BF16)
    return (lower, upper), (t_lower, t_upper)


def _shift_rows(x, k):
    if k == 0:
        return x
    n = x.shape[0]
    rows = lax.broadcasted_iota(jnp.int32, x.shape, 0)
    y = pltpu.roll(x, k % n, axis=0)
    if k > 0:
        return jnp.where(rows >= k, y, 0.0)
    return jnp.where(rows < n + k, y, 0.0)


def _dwconv(x, w, b):
    taps = w.shape[0]
    pad = (taps - 1) // 2
    acc = x * w[pad:pad + 1, :] + b
    for j in range(taps):
        if j != pad:
            acc = acc + _shift_rows(x, pad - j) * w[j:j + 1, :]
    return acc


CONV_HALO = 8


def _conv_blocks(src_ref, pad_ref, w, b, rb, emit):
    seq = src_ref.shape[0]
    taps = w.shape[0]
    pad = (taps - 1) // 2
    zeros = jnp.zeros((CONV_HALO, pad_ref.shape[1]), F32)
    pad_ref[0:CONV_HALO, :] = zeros
    pad_ref[CONV_HALO + seq:CONV_HALO + seq + CONV_HALO, :] = zeros
    pad_ref[CONV_HALO:CONV_HALO + seq, :] = src_ref[...].astype(F32)
    for r0 in range(0, seq, rb):
        acc = b
        for j in range(taps):
            lo = CONV_HALO + r0 + j - pad
            acc = acc + pad_ref[lo:lo + rb, :] * w[j:j + 1, :]
        emit(r0, acc)


def _rms(x):
    return x * lax.rsqrt(jnp.mean(x * x, axis=-1, keepdims=True) + EPS)


def _inproj_kernel(x_ref, g_ref, w_ref, o_ref, xn_ref):
    @pl.when(pl.program_id(1) == 0)
    def _():
        xn_ref[...] = (_rms(x_ref[...]) * g_ref[...]).astype(BF16)

    o_ref[...] = _dot(xn_ref[...], w_ref[...]).astype(o_ref.dtype)


def _inproj(x2d, g, w, out_dtype, tn):
    m, k = x2d.shape
    n = w.shape[1]
    tm = min(1024, m)
    return pl.pallas_call(
        _inproj_kernel,
        grid=(m // tm, n // tn),
        in_specs=[pl.BlockSpec((tm, k), lambda i, j: (i, 0)),
                  pl.BlockSpec((1, k), lambda i, j: (0, 0)),
                  pl.BlockSpec((k, tn), lambda i, j: (0, j))],
        out_specs=pl.BlockSpec((tm, tn), lambda i, j: (i, j)),
        out_shape=jax.ShapeDtypeStruct((m, n), out_dtype),
        scratch_shapes=[pltpu.VMEM((tm, k), BF16)],
        compiler_params=_params("parallel", "arbitrary"),
        name="inproj",
    )(x2d, g.reshape(1, k).astype(F32), w)


def _outproj_kernel(y_ref, w_ref, r_ref, g_ref, o_ref, *, final):
    acc = _dot(y_ref[...], w_ref[...]) + r_ref[...]
    if final:
        acc = _rms(acc) * g_ref[...]
    o_ref[...] = acc


def _outproj(y2d, w, res2d, final_g=None):
    m, k = y2d.shape
    n = w.shape[1]
    tm = min(512, m)
    final = final_g is not None
    g = (final_g if final else jnp.ones((n,), F32)).reshape(1, n).astype(F32)
    return pl.pallas_call(
        functools.partial(_outproj_kernel, final=final),
        grid=(m // tm,),
        in_specs=[pl.BlockSpec((tm, k), lambda i: (i, 0)),
                  pl.BlockSpec((k, n), lambda i: (0, 0)),
                  pl.BlockSpec((tm, n), lambda i: (i, 0)),
                  pl.BlockSpec((1, n), lambda i: (0, 0))],
        out_specs=pl.BlockSpec((tm, n), lambda i: (i, 0)),
        out_shape=jax.ShapeDtypeStruct((m, n), F32),
        compiler_params=_params("parallel"),
        name="outproj",
    )(y2d, w.astype(BF16), res2d, g)


def _pad_cols(w, n):
    return jnp.pad(w, ((0, 0), (0, n - w.shape[1])))


def _ssd_kernel(z_ref, x_ref, b_ref, c_ref, dtc_ref, dtr_ref,
                cwx_ref, cwb_ref, cwc_ref, cbx_ref, cbb_ref, cbc_ref,
                biasr_ref, biasc_ref, alogr_ref, alogc_ref, dskip_ref, gnorm_ref,
                o_ref, xs_ref, cs_ref, bt_ref, y_ref, s_ref, u_ref, e1_ref, dec_ref, pad_ref):
    q = SSD_CHUNK
    seq = x_ref.shape[0]
    nc = seq // q
    hpg = SSD_HPG
    assert hpg * SSD_HEAD_DIM == 2 * LANES
    (lower, upper), (t_lower, t_upper) = _tri_consts(q)
    lane_head = lax.shift_right_logical(
        lax.broadcasted_iota(jnp.int32, (1, SSD_GW), 1), int(math.log2(SSD_HEAD_DIM)))
    first_head = lax.broadcasted_iota(jnp.int32, (1, LANES), 1) < SSD_HEAD_DIM
    fwd_lane = lax.broadcasted_iota(jnp.int32, (1, 2 * hpg), 1) < hpg
    fwd_row = lax.broadcasted_iota(jnp.int32, (2 * hpg, 1), 0) < hpg

    xs_ref[...] = _silu(_dwconv(x_ref[...].astype(F32), cwx_ref[...], cbx_ref[...]))
    cs_ref[...] = _silu(_dwconv(c_ref[...].astype(F32), cwc_ref[...], cbc_ref[...])).astype(BF16)
    bs = _silu(_dwconv(b_ref[...].astype(F32), cwb_ref[...], cbb_ref[...]))
    for c in range(nc):
        bt_ref[c] = bs[c * q:(c + 1) * q, :].T.astype(BF16)

    bias_row = biasr_ref[...]
    bias_col = biasc_ref[...]
    a_row = -jnp.exp(alogr_ref[...])
    a_col = -jnp.exp(alogc_ref[...])
    dskip = dskip_ref[...]

    def rep(cols, j):
        return jnp.broadcast_to(cols[:, j:j + 1], (cols.shape[0], LANES))

    sel_dt = jnp.where(
        lax.broadcasted_iota(jnp.int32, (2 * hpg, 2 * SSD_GW), 0)
        == lax.shift_right_logical(lax.broadcasted_iota(jnp.int32, (2 * hpg, 2 * SSD_GW), 1),
                                   int(math.log2(SSD_HEAD_DIM))), 1.0, 0.0).astype(BF16)

    unroll = 2 if nc % 2 == 0 else 1

    def local_body(i, carry):
        chunks = [i * unroll + j for j in range(unroll)]
        rows = [pl.ds(pl.multiple_of(c * q, q), q) for c in chunks]
        us = range(unroll)
        dt_col = [_softplus(dtc_ref[r, :] + bias_row) for r in rows]
        dt_row = [_softplus(dtr_ref[c] + bias_col) for c in chunks]
        la_col = [dt * a_row for dt in dt_col]
        la_row = [dt * a_col for dt in dt_row]
        pre_col = [_tri_dot(t_lower, la) for la in la_col]
        pre_row = [_dot_tri(la, t_upper) for la in la_row]
        cbs = [_dot(cs_ref[rows[j], :], bt_ref[chunks[j]]) for j in us]
        dt_exp = [_dot(dt.astype(BF16), sel_dt) for dt in dt_col]
        ys, xqs, cum_exps = [], [], []
        for j in us:
            cum_col = jnp.where(fwd_lane, pre_col[j], pre_col[j][q - 1:q, :] - pre_col[j] + la_col[j])
            cum_row = jnp.where(fwd_row, pre_row[j], pre_row[j][:, q - 1:q] - pre_row[j] + la_row[j])
            reps = [rep(cum_col, r) for r in range(2 * hpg)]
            ms = []
            for r in range(hpg):
                rb = hpg + r
                mf = jnp.exp(jnp.where(lower, reps[r] - cum_row[r:r + 1, :], _NEG_INF)) * dt_row[j][r:r + 1, :]
                mb = jnp.exp(jnp.where(upper, reps[rb] - cum_row[rb:rb + 1, :], _NEG_INF)) * dt_row[j][rb:rb + 1, :]
                ms.append((cbs[j] * (mf + mb)).astype(BF16))
            xq = xs_ref[rows[j], :]
            xb = xq.astype(BF16)
            xbd = jnp.concatenate(
                [jnp.where(lane_head == r, xb, jnp.zeros_like(xb)) for r in range(hpg)], axis=0)
            ys.append(_dot(jnp.concatenate(ms, axis=1), xbd))
            xqs.append(xq)
            cum_exps.append([jnp.concatenate(
                [jnp.where(first_head, reps[lo], reps[lo + 1]),
                 jnp.where(first_head, reps[lo + 2], reps[lo + 3])], axis=1) for lo in (0, hpg)])
        for j in us:
            y_ref[rows[j], :] = ys[j] + xqs[j] * dskip
            for d in range(2):
                cum_exp = cum_exps[j][d]
                tot = cum_exp[q - 1:q, :] if d == 0 else cum_exp[0:1, :]
                w = dt_exp[j][:, d * SSD_GW:(d + 1) * SSD_GW] * jnp.exp(tot - cum_exp)
                u_ref[d, chunks[j]] = _dot(bt_ref[chunks[j]], (xqs[j] * w).astype(BF16))
                e1_ref[d, rows[j], :] = jnp.exp(cum_exp).astype(BF16)
                dec_ref[d, chunks[j]] = jnp.exp(tot)
        return carry

    lax.fori_loop(0, nc // unroll, local_body, 0)

    s_ref[...] = jnp.zeros_like(s_ref)

    def serial(i, carry):
        chunks = (i, nc - 1 - i)
        rows = [pl.ds(pl.multiple_of(c * q, q), q) for c in chunks]
        states = [s_ref[d] for d in range(2)]
        inter = [_dot(cs_ref[rows[d], :], states[d].astype(BF16)) for d in range(2)]
        for d in range(2):
            s_ref[d] = states[d] * dec_ref[d, chunks[d]] + u_ref[d, chunks[d]]
        for d in range(2):
            y_ref[rows[d], :] += inter[d] * e1_ref[d, rows[d], :].astype(F32)
        return carry

    lax.fori_loop(0, nc, serial, 0)

    y = y_ref[...] * _silu(z_ref[...].astype(F32))
    o_ref[...] = (_rms(y) * gnorm_ref[...]).astype(o_ref.dtype)


def _ssd_layer(h, norm, w_in, conv_w, conv_b, dt_bias, a_log, d_skip, gnorm, w_out, final_g):
    bsz, seq, _ = h.shape
    g, hpg, q = SSD_GROUPS, SSD_HPG, SSD_CHUNK
    nc = seq // q
    h2 = h.reshape(bsz * seq, D_MODEL)
    n_main = 2 * D_INNER + 2 * SSD_BC
    pm = _inproj(h2, norm, w_in[:, :n_main].astype(BF16), BF16, 1024).reshape(bsz, seq, n_main)
    gates = _inproj(h2, norm, _pad_cols(w_in[:, n_main:], LANES).astype(BF16), F32, LANES)
    dt = gates[:, :2 * SSD_HEADS].reshape(bsz, seq, 2, g, hpg)
    dt_col = dt.transpose(0, 3, 1, 2, 4).reshape(bsz, g, seq, 2 * hpg)
    dt_row = dt.reshape(bsz, nc, q, 2, g, hpg).transpose(0, 4, 1, 3, 5, 2).reshape(bsz, g, nc, 2 * hpg, q)

    def per_group(p):
        return p.astype(F32).reshape(2, g, hpg).transpose(1, 0, 2).reshape(g, 2 * hpg)

    bias, alog = per_group(dt_bias), per_group(a_log)
    dskip = jnp.repeat(d_skip.astype(F32).reshape(g, hpg), SSD_HEAD_DIM, axis=1).reshape(g, 1, SSD_GW)
    cw = conv_w.astype(F32)
    cb = conv_b.astype(F32).reshape(1, -1)
    taps = cw.shape[0]
    xo = D_INNER // SSD_GW
    bo = 2 * D_INNER // SSD_STATE
    co = bo + SSD_GROUPS
    cbo = D_INNER // SSD_STATE
    cco = cbo + SSD_GROUPS

    y = pl.pallas_call(
        _ssd_kernel,
        grid=(bsz, g),
        in_specs=[
            pl.BlockSpec((None, seq, SSD_GW), lambda b, j: (b, 0, j)),
            pl.BlockSpec((None, seq, SSD_GW), lambda b, j: (b, 0, xo + j)),
            pl.BlockSpec((None, seq, SSD_STATE), lambda b, j: (b, 0, bo + j)),
            pl.BlockSpec((None, seq, SSD_STATE), lambda b, j: (b, 0, co + j)),
            pl.BlockSpec((None, None, seq, 2 * hpg), lambda b, j: (b, j, 0, 0)),
            pl.BlockSpec((None, None, nc, 2 * hpg, q), lambda b, j: (b, j, 0, 0, 0)),
            pl.BlockSpec((taps, SSD_GW), lambda b, j: (0, j)),
            pl.BlockSpec((taps, SSD_STATE), lambda b, j: (0, cbo + j)),
            pl.BlockSpec((taps, SSD_STATE), lambda b, j: (0, cco + j)),
            pl.BlockSpec((1, SSD_GW), lambda b, j: (0, j)),
            pl.BlockSpec((1, SSD_STATE), lambda b, j: (0, cbo + j)),
            pl.BlockSpec((1, SSD_STATE), lambda b, j: (0, cco + j)),
            pl.BlockSpec((None, 1, 2 * hpg), lambda b, j: (j, 0, 0)),
            pl.BlockSpec((None, 2 * hpg, 1), lambda b, j: (j, 0, 0)),
            pl.BlockSpec((None, 1, 2 * hpg), lambda b, j: (j, 0, 0)),
            pl.BlockSpec((None, 2 * hpg, 1), lambda b, j: (j, 0, 0)),
            pl.BlockSpec((None, 1, SSD_GW), lambda b, j: (j, 0, 0)),
            pl.BlockSpec((None, 1, SSD_GW), lambda b, j: (j, 0, 0)),
        ],
        out_specs=pl.BlockSpec((None, seq, SSD_GW), lambda b, j: (b, 0, j)),
        out_shape=jax.ShapeDtypeStruct((bsz, seq, D_INNER), BF16),
        scratch_shapes=[pltpu.VMEM((seq, SSD_GW), F32),
                        pltpu.VMEM((seq, SSD_STATE), BF16),
                        pltpu.VMEM((nc, SSD_STATE, q), BF16),
                        pltpu.VMEM((seq, SSD_GW), F32),
                        pltpu.VMEM((2, SSD_STATE, SSD_GW), F32),
                        pltpu.VMEM((2, nc, SSD_STATE, SSD_GW), F32),
                        pltpu.VMEM((2, seq, SSD_GW), BF16),
                        pltpu.VMEM((2, nc, 1, SSD_GW), F32),
                        pltpu.VMEM((seq + 2 * CONV_HALO, SSD_GW), F32)],
        compiler_params=_params("parallel", "parallel"),
        name="ssd_mixer",
    )(pm, pm, pm, pm, dt_col, dt_row, cw, cw, cw, cb, cb, cb,
      bias.reshape(g, 1, 2 * hpg), bias.reshape(g, 2 * hpg, 1),
      alog.reshape(g, 1, 2 * hpg), alog.reshape(g, 2 * hpg, 1),
      dskip, gnorm.astype(F32).reshape(g, 1, SSD_GW))
    out = _outproj(y.reshape(bsz * seq, D_INNER), w_out, h2, final_g)
    return out.reshape(bsz, seq, D_MODEL)


def _gla_kernel(q_ref, k_ref, v_ref, z_ref, gl_ref, wg_ref, bg_ref, onorm_ref,
                o_ref, lg_ref, acc_ref, s_ref, qg_ref, u_ref, dec_ref):
    q = GLA_CHUNK
    seq = q_ref.shape[0]
    nc = seq // q
    masks, tris = _tri_consts(q)
    gl = gl_ref[...]
    for d in range(2):
        lg_ref[d] = _log_sigmoid(_dot_tri(gl, wg_ref[d]) + bg_ref[d]) * (1.0 / GLA_NORMALIZER)

    unroll = 2 if nc % 2 == 0 else 1

    def local_body(i, carry):
        chunks = [i * unroll + j for j in range(unroll)]
        rows = [pl.ds(pl.multiple_of(c * q, q), q) for c in chunks]
        pairs = [(j, d) for j in range(unroll) for d in range(2)]
        cums = {(j, d): _tri_dot(tris[d], lg_ref[d, rows[j], :]) for j, d in pairs}
        qgs, kgs, kds = {}, {}, {}
        for j, d in pairs:
            cum = cums[j, d]
            qc = q_ref[rows[j], :].astype(F32) * (GLA_DK ** -0.5)
            kc = k_ref[rows[j], :].astype(F32)
            tot = cum[q - 1:q, :] if d == 0 else cum[0:1, :]
            qgs[j, d] = (qc * jnp.exp(cum)).astype(BF16)
            kgs[j, d] = (kc * jnp.exp(-cum)).astype(BF16)
            kds[j, d] = (kc * jnp.exp(tot - cum)).astype(BF16)
            qg_ref[d, rows[j], :] = qgs[j, d]
            dec_ref[d, chunks[j]] = jnp.exp(tot)
        atts = {p: _dot_nt(qgs[p], kgs[p]) for p in pairs}
        for j, d in pairs:
            u_ref[d, chunks[j]] = _dot_tn(v_ref[rows[j], :], kds[j, d]).astype(BF16)
        atts = {(j, d): jnp.where(masks[d], atts[j, d], 0.0).astype(BF16) for j, d in pairs}
        outs = {(j, d): _dot(atts[j, d], v_ref[rows[j], :]) for j, d in pairs}
        for j in range(unroll):
            acc_ref[rows[j], :] = outs[j, 0] + outs[j, 1]
        return carry

    lax.fori_loop(0, nc // unroll, local_body, 0)

    s_ref[...] = jnp.zeros_like(s_ref)

    def serial(i, carry):
        chunks = (i, nc - 1 - i)
        rows = [pl.ds(pl.multiple_of(c * q, q), q) for c in chunks]
        states = [s_ref[d] for d in range(2)]
        inter = [_dot_nt(qg_ref[d, rows[d], :], states[d].astype(BF16)) for d in range(2)]
        for d in range(2):
            s_ref[d] = states[d] * dec_ref[d, chunks[d]] + u_ref[d, chunks[d]].astype(F32)
        for d in range(2):
            acc_ref[rows[d], :] += inter[d]
        return carry

    lax.fori_loop(0, nc, serial, 0)

    out = _rms(acc_ref[...]) * onorm_ref[...]
    o_ref[...] = (out * _silu(z_ref[...].astype(F32))).astype(o_ref.dtype)


def _gla_layer(h, norm, w_in, w_gate, b_gate, onorm, w_out, final_g):
    bsz, seq, _ = h.shape
    nh = GLA_HEADS
    h2 = h.reshape(bsz * seq, D_MODEL)
    n_main = 2 * GLA_QK + 2 * D_INNER
    pm = _inproj(h2, norm, w_in[:, :n_main].astype(BF16), BF16, 1024).reshape(bsz, seq, n_main)
    gates = _inproj(h2, norm, _pad_cols(w_in[:, n_main:], LANES).astype(BF16), F32, LANES)
    gates = gates.reshape(bsz, seq, LANES)
    wg = jnp.zeros((2, LANES, GLA_QK), F32)
    for d in range(2):
        wg = wg.at[d, d * GLA_RANK:(d + 1) * GLA_RANK, :].set(w_gate[d].astype(F32))
    vo = 2 * GLA_QK // GLA_DV
    zo = vo + nh
    y = pl.pallas_call(
        _gla_kernel,
        grid=(bsz, nh),
        in_specs=[
            pl.BlockSpec((None, seq, GLA_DK), lambda b, j: (b, 0, j)),
            pl.BlockSpec((None, seq, GLA_DK), lambda b, j: (b, 0, nh + j)),
            pl.BlockSpec((None, seq, GLA_DV), lambda b, j: (b, 0, vo + j)),
            pl.BlockSpec((None, seq, GLA_DV), lambda b, j: (b, 0, zo + j)),
            pl.BlockSpec((None, seq, LANES), lambda b, j: (b, 0, 0)),
            pl.BlockSpec((2, LANES, GLA_DK), lambda b, j: (0, 0, j)),
            pl.BlockSpec((2, 1, GLA_DK), lambda b, j: (0, 0, j)),
            pl.BlockSpec((1, GLA_DV), lambda b, j: (0, 0)),
        ],
        out_specs=pl.BlockSpec((None, seq, GLA_DV), lambda b, j: (b, 0, j)),
        out_shape=jax.ShapeDtypeStruct((bsz, seq, D_INNER), BF16),
        scratch_shapes=[pltpu.VMEM((2, seq, GLA_DK), F32),
                        pltpu.VMEM((seq, GLA_DV), F32),
                        pltpu.VMEM((2, GLA_DV, GLA_DK), F32),
                        pltpu.VMEM((2, seq, GLA_DK), BF16),
                        pltpu.VMEM((2, seq // GLA_CHUNK, GLA_DV, GLA_DK), BF16),
                        pltpu.VMEM((2, seq // GLA_CHUNK, 1, GLA_DK), F32)],
        compiler_params=_params("parallel", "parallel"),
        name="gla_mixer",
    )(pm, pm, pm, pm, gates, wg.astype(BF16), b_gate.astype(F32).reshape(2, 1, GLA_QK),
      onorm.astype(F32).reshape(1, GLA_DV))
    out = _outproj(y.reshape(bsz * seq, D_INNER), w_out, h2, final_g)
    return out.reshape(bsz, seq, D_MODEL)


def _hy_mlp_kernel(f_ref, w0_ref, b0_ref, w1_ref, b1_ref, w2_ref, b2_ref, fr_ref, o_ref):
    fr = fr_ref[...]
    hid = jnp.sin(fr[0:1, :] * (_dot_f32(f_ref[...], w0_ref[...]) + b0_ref[...]))
    hid = jnp.sin(fr[1:2, :] * (_dot_f32(hid, w1_ref[...]) + b1_ref[...]))
    hid = jnp.sin(fr[2:3, :] * (_dot_f32(hid, w2_ref[...]) + b2_ref[...]))
    o_ref[...] = hid


def _hy_spec_kernel(hid_ref, wf_ref, wb_ref, t_ref, dl_ref, cos_ref, sin_ref, cosr_ref, sinr_ref,
                    kre_ref, ks_ref, p_ref):
    lb = hid_ref.shape[0] // 2

    @pl.when(pl.program_id(2) == 0)
    def _():
        env = jnp.exp(-t_ref[...] * dl_ref[...])
        hf = _dot_f32(hid_ref[...], wf_ref[...]) * env
        hb = _dot_f32(hid_ref[...], wb_ref[...]) * env
        first = lax.broadcasted_iota(jnp.int32, (lb, hf.shape[1]), 0) == 0
        hf_lo0 = jnp.where(first, 0.0, hf[0:lb])
        hb_lo0 = jnp.where(first, 0.0, hb[0:lb])
        p_ref[0] = (hf[0:lb] + hb_lo0).astype(BF16)
        p_ref[1] = (hf[0:lb] - hb_lo0).astype(BF16)
        p_ref[2] = hf[lb:].astype(BF16)
        p_ref[3] = hf_lo0.astype(BF16)
        p_ref[4] = hb[lb:].astype(BF16)
        p_ref[5] = hb_lo0.astype(BF16)

    scale = 1.0 / lb
    cos, sin, cosr, sinr = cos_ref[...], sin_ref[...], cosr_ref[...], sinr_ref[...]
    dt = kre_ref.dtype
    kre_ref[0] = (_dot(cos, p_ref[0]) * scale).astype(dt)
    ks_ref[0] = (_dot(sin, p_ref[1]) * scale).astype(dt)
    kre_ref[1] = ((_dot(cos, p_ref[2]) + _dot(cosr, p_ref[3])) * scale).astype(dt)
    ks_ref[1] = ((_dot(sin, p_ref[2]) + _dot(sinr, p_ref[3])) * scale).astype(dt)
    kre_ref[2] = ((_dot(cos, p_ref[4]) + _dot(cosr, p_ref[5])) * scale).astype(dt)
    ks_ref[2] = ((_dot(sin, p_ref[4]) + _dot(sinr, p_ref[5])) * (-scale)).astype(dt)


def _hy_conv_kernel(v_ref, x1_ref, x2_ref, z_ref, cwv_ref, cw1_ref, cw2_ref,
                    cbv_ref, cb1_ref, cb2_ref, kre_ref, ks_ref, d_ref, w_ref, o_ref,
                    y_ref, yb_ref, zf_ref, g_ref, *, fb):
    seq, tc = v_ref.shape
    lb = seq // 2
    nblk = lb // fb
    y_ref[...] = _dwconv(v_ref[...].astype(F32), cwv_ref[...], cbv_ref[...])
    gates = ((x1_ref, cw1_ref, cb1_ref), (x2_ref, cw2_ref, cb2_ref))
    for o, (x_ref, cw_ref, cb_ref) in enumerate(gates):
        yb_ref[:, 0:tc] = y_ref[0:lb, :].astype(BF16)
        yb_ref[:, tc:2 * tc] = y_ref[lb:seq, :].astype(BF16)
        g_ref[...] = _dwconv(x_ref[...].astype(F32), cw_ref[...], cb_ref[...])

        def forward(k, carry, o=o):
            f0 = pl.multiple_of(k * fb, fb)
            yb = yb_ref[...]
            yre = _dot(w_ref[pl.ds(f0, fb), :], yb)
            yim = _dot(w_ref[pl.ds(lb + f0, fb), :], yb)
            re = [yre[:, 0:tc], yre[:, tc:2 * tc]]
            im = [yim[:, 0:tc], yim[:, tc:2 * tc]]
            kre = [kre_ref[o, n, pl.ds(f0, fb), :].astype(F32) for n in range(3)]
            ks = [ks_ref[o, n, pl.ds(f0, fb), :].astype(F32) for n in range(3)]
            for i, (n0, n1) in enumerate(((0, 2), (1, 0))):
                zre = re[0] * kre[n0] - im[0] * ks[n0] + re[1] * kre[n1] - im[1] * ks[n1]
                zim = im[0] * kre[n0] + re[0] * ks[n0] + im[1] * kre[n1] + re[1] * ks[n1]
                zf_ref[pl.ds(f0, fb), i * tc:(i + 1) * tc] = zre.astype(BF16)
                zf_ref[pl.ds(lb + f0, fb), i * tc:(i + 1) * tc] = zim.astype(BF16)
            return carry

        lax.fori_loop(0, nblk, forward, 0)

        def inverse(k, carry, o=o):
            t0 = pl.multiple_of(k * fb, fb)
            conv = (_dot(w_ref[pl.ds(t0, fb), :], zf_ref[0:lb, :])
                    + _dot(w_ref[pl.ds(lb + t0, fb), :], zf_ref[lb:2 * lb, :]))
            for i in range(2):
                rows = pl.ds(i * lb + t0, fb)
                y_ref[rows, :] = g_ref[rows, :] * (conv[:, i * tc:(i + 1) * tc]
                                                   + y_ref[rows, :] * d_ref[o:o + 1, :])
            return carry

        lax.fori_loop(0, nblk, inverse, 0)
    o_ref[...] = (y_ref[...] * _silu(z_ref[...].astype(F32))).astype(o_ref.dtype)


def _hyena_layer(h, norm, w_in, conv_w, conv_b, ffn_w_in, ffn_b_in, ffn_w_hid, ffn_b_hid,
                 ffn_freq, ffn_w_out, d_bias, w_out, final_g):
    bsz, seq, _ = h.shape
    c = D_INNER
    h2 = h.reshape(bsz * seq, D_MODEL)
    pm = _inproj(h2, norm, w_in.astype(BF16), BF16, 1024).reshape(bsz, seq, 4 * c)

    t = jnp.linspace(0.0, 1.0, seq, dtype=F32)[:, None]
    pos = jnp.arange(seq, dtype=F32)[:, None]
    bands = jnp.linspace(1e-4, HY_BANDS - 1, HY_BANDS, dtype=F32)[None]
    ang = (2.0 * math.pi / seq) * pos * bands
    feats = jnp.concatenate([t, jnp.cos(ang), -jnp.sin(ang)], axis=-1)
    feats = _pad_cols(feats, LANES)
    w0 = jnp.pad(ffn_w_in.astype(F32), ((0, LANES - HY_EMB), (0, 0)))
    hid = pl.pallas_call(
        _hy_mlp_kernel,
        out_shape=jax.ShapeDtypeStruct((seq, HY_FFN), F32),
        name="hyena_filter_mlp",
    )(feats, w0, ffn_b_in.astype(F32).reshape(1, HY_FFN),
      ffn_w_hid[0].astype(F32), ffn_b_hid[0].astype(F32).reshape(1, HY_FFN),
      ffn_w_hid[1].astype(F32), ffn_b_hid[1].astype(F32).reshape(1, HY_FFN),
      ffn_freq.astype(F32))

    max_decay = math.log(HY_TARGET) / HY_FAST_PCT
    min_decay = math.log(HY_TARGET) / HY_SLOW_PCT
    deltas = jnp.abs(jnp.linspace(min_decay, max_decay, c, dtype=F32)).reshape(1, c)

    lb = seq // 2
    fi = jnp.arange(lb, dtype=jnp.int32)
    n4 = 8 * lb

    def table(num, fn):
        return fn((2.0 * math.pi / n4) * (num % n4).astype(F32)).astype(BF16)

    num1 = (2 * fi[:, None] + 1) * (2 * fi[None, :])
    numr = (2 * fi[:, None] + 1) * (2 * (fi[None, :] - lb))
    num2 = (2 * fi[:, None] + 1) * (2 * fi[None, :] + 1)
    cos1, sin1 = table(num1, jnp.cos), table(num1, jnp.sin)
    cosr, sinr = table(numr, jnp.cos), table(numr, jnp.sin)
    w2 = jnp.concatenate([table(num2, jnp.cos), table(num2, jnp.sin)], axis=0)

    tc = min(512, c)
    fb = min(512, lb)
    nct = c // tc
    wo = ffn_w_out.astype(F32)
    tab_spec = pl.BlockSpec((fb, lb), lambda o, j, k: (k, 0))
    kre, ks = pl.pallas_call(
        _hy_spec_kernel,
        grid=(HY_ORDER, nct, lb // fb),
        in_specs=[
            pl.BlockSpec((seq, HY_FFN), lambda o, j, k: (0, 0)),
            pl.BlockSpec((HY_FFN, tc), lambda o, j, k: (0, o * nct + j)),
            pl.BlockSpec((HY_FFN, tc), lambda o, j, k: (0, (HY_ORDER + o) * nct + j)),
            pl.BlockSpec((seq, 1), lambda o, j, k: (0, 0)),
            pl.BlockSpec((1, tc), lambda o, j, k: (0, j)),
            tab_spec, tab_spec, tab_spec, tab_spec,
        ],
        out_specs=[pl.BlockSpec((None, 3, fb, tc), lambda o, j, k: (o, 0, k, j)),
                   pl.BlockSpec((None, 3, fb, tc), lambda o, j, k: (o, 0, k, j))],
        out_shape=[jax.ShapeDtypeStruct((HY_ORDER, 3, lb, c), BF16)] * 2,
        scratch_shapes=[pltpu.VMEM((6, lb, tc), BF16)],
        compiler_params=_params("parallel", "parallel", "arbitrary"),
        name="hyena_filter_spectrum",
    )(hid, wo, wo, t, deltas, cos1, sin1, cosr, sinr)

    tcc = min(256, c)
    ncc = c // tcc
    cw = conv_w.astype(F32)
    cb = conv_b.astype(F32).reshape(1, -1)
    taps = cw.shape[0]
    y = pl.pallas_call(
        functools.partial(_hy_conv_kernel, fb=fb),
        grid=(ncc, bsz),
        in_specs=[
            pl.BlockSpec((None, seq, tcc), lambda j, b: (b, 0, j)),
            pl.BlockSpec((None, seq, tcc), lambda j, b: (b, 0, ncc + j)),
            pl.BlockSpec((None, seq, tcc), lambda j, b: (b, 0, 2 * ncc + j)),
            pl.BlockSpec((None, seq, tcc), lambda j, b: (b, 0, 3 * ncc + j)),
            pl.BlockSpec((taps, tcc), lambda j, b: (0, j)),
            pl.BlockSpec((taps, tcc), lambda j, b: (0, ncc + j)),
            pl.BlockSpec((taps, tcc), lambda j, b: (0, 2 * ncc + j)),
            pl.BlockSpec((1, tcc), lambda j, b: (0, j)),
            pl.BlockSpec((1, tcc), lambda j, b: (0, ncc + j)),
            pl.BlockSpec((1, tcc), lambda j, b: (0, 2 * ncc + j)),
            pl.BlockSpec((HY_ORDER, 3, lb, tcc), lambda j, b: (0, 0, 0, j), pipeline_mode=pl.Buffered(1)),
            pl.BlockSpec((HY_ORDER, 3, lb, tcc), lambda j, b: (0, 0, 0, j), pipeline_mode=pl.Buffered(1)),
            pl.BlockSpec((HY_ORDER, tcc), lambda j, b: (0, j)),
            pl.BlockSpec((2 * lb, lb), lambda j, b: (0, 0), pipeline_mode=pl.Buffered(1)),
        ],
        out_specs=pl.BlockSpec((None, seq, tcc), lambda j, b: (b, 0, j)),
        out_shape=jax.ShapeDtypeStruct((bsz, seq, c), BF16),
        scratch_shapes=[pltpu.VMEM((seq, tcc), F32),
                        pltpu.VMEM((lb, 2 * tcc), BF16),
                        pltpu.VMEM((2 * lb, 2 * tcc), BF16),
                        pltpu.VMEM((seq, tcc), F32)],
        compiler_params=_params("parallel", "parallel"),
        name="hyena_mixer",
    )(pm, pm, pm, pm, cw, cw, cw, cb, cb, cb, kre, ks, d_bias.astype(F32), w2)
    out = _outproj(y.reshape(bsz * seq, c), w_out, h2, final_g)
    return out.reshape(bsz, seq, D_MODEL)


def _mlstm_kernel(xm_ref, z_ref, og_ref, gc_ref, gr_ref, cw_ref, cb_ref, wq_ref, wk_ref, wv_ref,
                  gbr_ref, gbc_ref, skip_ref, onorm_ref,
                  o_ref, q_ref, k_ref, v_ref, ch_ref, acc_ref, c_ref, nl_ref, st_ref, kt_ref, str_ref, cmat_ref,
                  inc_ref):
    q = ML_CHUNK
    seq = xm_ref.shape[0]
    nc = seq // q
    masks, tris = _tri_consts(q)

    for j in range(ML_DH // LANES):
        sl = slice(j * LANES, (j + 1) * LANES)
        ch_ref[:, sl] = _silu(_dwconv(xm_ref[:, sl].astype(F32), cw_ref[:, sl], cb_ref[:, sl])).astype(BF16)
    ch = ch_ref[...]
    q_ref[...] = _dot(ch, wq_ref[...]).astype(BF16)
    kf = _dot(ch, wk_ref[...]) * (ML_DK ** -0.5)
    k_ref[...] = kf.astype(BF16)
    for c in range(nc):
        kt_ref[c] = kf[c * q:(c + 1) * q, :].T.astype(BF16)
    v_ref[...] =_dot(xm_ref[...], wv_ref[...]).astype(BF16)

    gb_row = gbr_ref[...]
    gb_col = gbc_ref[...]

    (t_lower, t_upper) = tris
    stat_lane = lax.broadcasted_iota(jnp.int32, (1, 8), 1)

    unroll = 2 if nc % 2 == 0 else 1

    def local_body(i, carry):
        chunks = [i * unroll + j for j in range(unroll)]
        rows = [pl.ds(pl.multiple_of(c * q, q), q) for c in chunks]
        pairs = [(j, d) for j in range(unroll) for d in range(2)]
        g_col = [gc_ref[r, :] + gb_row for r in rows]
        g_row = [gr_ref[c] + gb_col for c in chunks]
        lf_col = [_log_sigmoid(g) for g in g_col]
        lf_row = [_log_sigmoid(g) for g in g_row]
        tri_c, tri_r = (t_lower, t_upper), (t_upper, t_lower)
        b_cols = {(j, d): _tri_dot(tri_c[d], lf_col[j]) for j, d in pairs}
        b_rows = {(j, d): _dot_tri(lf_row[j], tri_r[d]) for j, d in pairs}
        qk = [_dot_nt(q_ref[r, :], k_ref[r, :]) for r in rows]
        ss = {}
        for j, d in pairs:
            i_col = g_col[j][:, 2 * d:2 * d + 1]
            i_row = g_row[j][2 * d:2 * d + 1, :]
            b_col = b_cols[j, d][:, 2 * d + 1:2 * d + 2]
            b_row = b_rows[j, d][2 * d + 1:2 * d + 2, :]
            logd = jnp.where(masks[d], b_col - b_row + i_row, _NEG_INF)
            m_loc = jnp.max(logd, axis=1, keepdims=True)
            s = qk[j] * jnp.exp(logd - m_loc)
            ss[j, d] = s.astype(BF16)
            den_loc = jnp.sum(s, axis=1, keepdims=True)
            tot = b_col[q - 1:q, :] if d == 0 else b_col[0:1, :]
            lw_col = tot - b_col + i_col
            lw_max = jnp.max(lw_col, axis=0, keepdims=True)
            stats = (m_loc, den_loc, b_col, lw_col - lw_max, jnp.broadcast_to(lw_max, (q, 1)))
            packed = stats[-1]
            for lane in range(len(stats) - 2, -1, -1):
                packed = jnp.where(stat_lane == lane, stats[lane], packed)
            st_ref[d, rows[j], :] = packed
            str_ref[d, chunks[j]] = jnp.broadcast_to(tot - b_row + i_row - lw_max, (8, q))
        for j, d in pairs:
            nl_ref[d, rows[j], :] = _dot(ss[j, d], v_ref[rows[j], :]).astype(BF16)
        return carry

    lax.fori_loop(0, nc // unroll, local_body, 0)

    c_ref[...] = jnp.zeros_like(c_ref)
    cmat_ref[...] = jnp.zeros_like(cmat_ref)
    acc_ref[...] = jnp.zeros_like(acc_ref)

    def serial(i, carry):
        chunks = (i, nc - 1 - i)
        rows = [pl.ds(pl.multiple_of(c * q, q), q) for c in chunks]
        w_rows = [jnp.exp(str_ref[d, chunks[d]]).astype(BF16) for d in range(2)]
        for d in range(2):
            inc_ref[d] = _dot(kt_ref[chunks[d]] * w_rows[d][0:1, :], v_ref[rows[d], :])
        ksums = [_dot(w_rows[d], k_ref[rows[d], :])[0:1, :] for d in range(2)]
        new, decs, gains = [], [], []
        for d in range(2):
            n_row, m = carry[d]
            st = st_ref[d, rows[d], :]
            m_loc, den_loc, b_col, lw_max = st[:, 0:1], st[:, 1:2], st[:, 2:3], st[0:1, 4:5]
            inter = b_col + m
            m_row = jnp.maximum(m_loc, inter)
            a_loc = jnp.exp(m_loc - m_row)
            a_int = jnp.exp(inter - m_row)
            tot = b_col[q - 1:q, :] if d == 0 else b_col[0:1, :]
            m_new = jnp.maximum(tot + m, lw_max)
            dec = jnp.exp(tot + m - m_new)
            gain = jnp.exp(lw_max - m_new)
            qc = q_ref[rows[d], :]
            den = den_loc * a_loc + jnp.sum(qc.astype(F32) * n_row, axis=1, keepdims=True) * a_int
            scale = 1.0 / jnp.maximum(jnp.abs(den), jnp.exp(-m_row))
            acc_ref[rows[d], :] += (nl_ref[d, rows[d], :].astype(F32) * (a_loc * scale)
                                    + _dot(qc, cmat_ref[d]) * (a_int * scale))
            decs.append(dec)
            gains.append(gain)
            new.append((n_row * dec + ksums[d] * gain, m_new))

        blk = 32
        for d in range(2):
            for r in range(0, ML_DK, blk):
                cnew = c_ref[d, r:r + blk, :] * decs[d] + inc_ref[d, r:r + blk, :] * gains[d]
                c_ref[d, r:r + blk, :] = cnew
                cmat_ref[d, r:r + blk, :] = cnew.astype(BF16)
        return tuple(new)

    init = (jnp.zeros((1, ML_DK), F32), jnp.full((1, 1), _NEG_INF, F32))
    lax.fori_loop(0, nc, serial, (init, init))

    hh = _sigmoid(og_ref[...].astype(F32)) * acc_ref[...]
    hh = _rms(hh) * onorm_ref[...] + skip_ref[...] * ch_ref[...].astype(F32)
    o_ref[...] = (hh * _silu(z_ref[...].astype(F32))).astype(o_ref.dtype)


def _mlstm_layer(h, norm, w_in, conv_w, conv_b, w_q, w_k, w_v, gate_b, skip, onorm, w_out, final_g):
    bsz, seq, _ = h.shape
    nh, q = ML_HEADS, ML_CHUNK
    nc = seq // q
    h2 = h.reshape(bsz * seq, D_MODEL)
    n_main = 3 * D_INNER
    pm = _inproj(h2, norm, w_in[:, :n_main].astype(BF16), BF16, 1024).reshape(bsz, seq, n_main)
    gates = _inproj(h2, norm, _pad_cols(w_in[:, n_main:], LANES).astype(BF16), F32, LANES)
    gt = gates[:, :4 * nh].reshape(bsz, seq, 2, 2, nh)
    g_col = gt.transpose(0, 4, 1, 2, 3).reshape(bsz, nh, seq, 4)
    g_row = gt.reshape(bsz, nc, q, 2, 2, nh).transpose(0, 5, 1, 3, 4, 2).reshape(bsz, nh, nc, 4, q)
    g_row = jnp.pad(g_row, ((0, 0), (0, 0), (0, 0), (0, 4), (0, 0)))
    gb =gate_b.astype(F32).transpose(2, 0, 1).reshape(nh, 4)
    taps = conv_w.shape[0]
    y = pl.pallas_call(
        _mlstm_kernel,
        grid=(bsz, nh),
        in_specs=[
            pl.BlockSpec((None, seq, ML_DH), lambda b, j: (b, 0, j)),
            pl.BlockSpec((None, seq, ML_DH), lambda b, j: (b, 0, nh + j)),
            pl.BlockSpec((None, seq, ML_DH), lambda b, j: (b, 0, 2 * nh + j)),
            pl.BlockSpec((None, None, seq, 4), lambda b, j: (b, j, 0, 0)),
            pl.BlockSpec((None, None, nc, 8, q), lambda b, j: (b, j, 0, 0, 0)),
            pl.BlockSpec((taps, ML_DH), lambda b, j: (0, j)),
            pl.BlockSpec((1, ML_DH), lambda b, j: (0, j)),
            pl.BlockSpec((None, ML_DH, ML_DK), lambda b, j: (j, 0, 0)),
            pl.BlockSpec((None, ML_DH, ML_DK), lambda b, j: (j, 0, 0)),
            pl.BlockSpec((None, ML_DH, ML_DH), lambda b, j: (j, 0, 0)),
            pl.BlockSpec((None, 1, 4), lambda b, j: (j, 0, 0)),
            pl.BlockSpec((None, 8, 1), lambda b, j: (j, 0, 0)),
            pl.BlockSpec((None, 1, ML_DH), lambda b, j: (j, 0, 0)),
            pl.BlockSpec((1, ML_DH), lambda b, j: (0, 0)),
        ],
        out_specs=pl.BlockSpec((None, seq, ML_DH), lambda b, j: (b, 0, j)),
        out_shape=jax.ShapeDtypeStruct((bsz, seq, D_INNER), BF16),
        scratch_shapes=[pltpu.VMEM((seq, ML_DK), BF16),
                        pltpu.VMEM((seq, ML_DK), BF16),
                        pltpu.VMEM((seq, ML_DH), BF16),
                        pltpu.VMEM((seq, ML_DH), BF16),
                        pltpu.VMEM((seq, ML_DH), F32),
                        pltpu.VMEM((2, ML_DK, ML_DH), F32),
                        pltpu.VMEM((2, seq, ML_DH), BF16),
                        pltpu.VMEM((2, seq, 8), F32),
                        pltpu.VMEM((nc, ML_DK, q), BF16),
                        pltpu.VMEM((2, nc, 8, q), F32),
                        pltpu.VMEM((2, ML_DK, ML_DH), BF16),
                        pltpu.VMEM((2, ML_DK, ML_DH), F32)],
        compiler_params=_params("parallel", "parallel"),
        name="mlstm_mixer",
    )(pm, pm, pm, g_col, g_row, conv_w.astype(F32), conv_b.astype(F32).reshape(1, -1),
      w_q.astype(BF16), w_k.astype(BF16), w_v.astype(BF16),
      gb.reshape(nh, 1, 4), jnp.pad(gb, ((0, 0), (0, 4))).reshape(nh, 8, 1),
      skip.astype(F32).reshape(nh, 1, ML_DH), onorm.astype(F32).reshape(1, ML_DH))
    out = _outproj(y.reshape(bsz * seq, D_INNER), w_out, h2, final_g)
    return out.reshape(bsz, seq, D_MODEL)


def kernel(x, ssd_norm, ssd_w_in, ssd_conv_w, ssd_conv_b, ssd_dt_bias, ssd_a_log, ssd_d, ssd_gnorm, ssd_w_out, gla_norm, gla_w_in, gla_w_gate, gla_b_gate, gla_onorm, gla_w_out, hy_norm, hy_w_in, hy_conv_w, hy_conv_b, hy_ffn_w_in, hy_ffn_b_in, hy_ffn_w_hid, hy_ffn_b_hid, hy_ffn_freq, hy_ffn_w_out, hy_d, hy_w_out, ml_norm, ml_w_in, ml_conv_w, ml_conv_b, ml_w_q, ml_w_k, ml_w_v, ml_gate_b, ml_skip, ml_onorm, ml_w_out, final_norm):
    depth = ssd_norm.shape[0] + gla_norm.shape[0] + hy_norm.shape[0] + ml_norm.shape[0]
    h = x
    for i in range(depth):
        kind, j = i % 4, i // 4
        fg = final_norm if i == depth - 1 else None
        if kind == 0:
            h = _ssd_layer(h, ssd_norm[j], ssd_w_in[j], ssd_conv_w[j], ssd_conv_b[j], ssd_dt_bias[j],
                           ssd_a_log[j], ssd_d[j], ssd_gnorm[j], ssd_w_out[j], fg)
        elif kind == 1:
            h = _gla_layer(h, gla_norm[j], gla_w_in[j], gla_w_gate[j], gla_b_gate[j], gla_onorm[j],
                           gla_w_out[j], fg)
        elif kind == 2:
            h = _hyena_layer(h, hy_norm[j], hy_w_in[j], hy_conv_w[j], hy_conv_b[j], hy_ffn_w_in[j],
                             hy_ffn_b_in[j], hy_ffn_w_hid[j], hy_ffn_b_hid[j], hy_ffn_freq[j],
                             hy_ffn_w_out[j], hy_d[j], hy_w_out[j], fg)
        else:
            h = _mlstm_layer(h, ml_norm[j], ml_w_in[j], ml_conv_w[j], ml_conv_b[j], ml_w_q[j], ml_w_k[j],
                             ml_w_v[j], ml_gate_b[j], ml_skip[j], ml_onorm[j], ml_w_out[j], fg)
    return h
```

```python
import functools
import math

import jax
import jax.numpy as jnp
from jax import lax
from jax.experimental import pallas as pl
from jax.experimental.pallas import tpu as pltpu

F32 = jnp.float32
BF16 = jnp.bfloat16

D_MODEL = 1024
D_INNER = 2 * D_MODEL
EPS = 1e-6

SSD_HEAD_DIM = 64
SSD_HEADS = D_INNER // SSD_HEAD_DIM
SSD_GROUPS = 8
SSD_HPG = SSD_HEADS // SSD_GROUPS
SSD_STATE = 128
SSD_CHUNK = 128
SSD_GW = SSD_HPG * SSD_HEAD_DIM
SSD_BC = SSD_GROUPS * SSD_STATE

GLA_HEADS = 4
GLA_DK = D_MODEL // 2 // GLA_HEADS
GLA_DV = D_INNER // GLA_HEADS
GLA_RANK = 16
GLA_NORMALIZER = 16.0
GLA_CHUNK = 64
GLA_QK = GLA_HEADS * GLA_DK

HY_ORDER = 2
HY_EMB = 33
HY_BANDS = (HY_EMB - 1) // 2
HY_FFN = 64
HY_INNER = 2
HY_FAST_PCT = 0.3
HY_SLOW_PCT = 1.5
HY_TARGET = 1e-2

ML_HEADS = 4
ML_DH = D_INNER // ML_HEADS
ML_DK = ML_DH // 2
ML_CHUNK = 128

LANES = 128
INPROJ_TM = 512
INPROJ_TN = 1024
VMEM_LIMIT = 56 * 1024 * 1024

_NEG_INF = float("-inf")


def _params(*sem):
    return pltpu.CompilerParams(dimension_semantics=sem, vmem_limit_bytes=VMEM_LIMIT)


def _sigmoid(x):
    return 0.5 * jnp.tanh(0.5 * x) + 0.5


def _silu(x):
    half = 0.5 * x
    return half + half * jnp.tanh(half)


def _softplus(x):
    return jnp.maximum(x, 0.0) + jnp.log1p(jnp.exp(-jnp.abs(x)))


def _log_sigmoid(x):
    return -_softplus(-x)


def _dot(a, b):
    return jnp.dot(a, b, preferred_element_type=F32)


def _dot_nt(a, b):
    return lax.dot_general(a, b, (((1,), (1,)), ((), ())), preferred_element_type=F32)


def _dot_tn(a, b):
    return lax.dot_general(a, b, (((0,), (0,)), ((), ())), preferred_element_type=F32)


def _dot_f32(a, b):
    return jnp.dot(a, b, preferred_element_type=F32, precision=lax.Precision.HIGHEST)


def _split(v):
    hi = v.astype(BF16)
    lo = (v - hi.astype(F32)).astype(BF16)
    return hi, lo


def _tri_dot(t, v):
    hi, lo = _split(v)
    return _dot(t, hi) + _dot(t, lo)


def _dot_tri(v, t):
    hi, lo = _split(v)
    return _dot(hi, t) + _dot(lo, t)


def _tri_consts(q):
    ri = lax.broadcasted_iota(jnp.int32, (q, q), 0)
    ci = lax.broadcasted_iota(jnp.int32, (q, q), 1)
    lower = ri >= ci
    upper = ci >= ri
    t_lower = jnp.where(lower, 1.0, 0.0).astype(BF16)
    t_upper = jnp.where(upper, 1.0, 0.0).astype(BF16)
    return (lower, upper), (t_lower, t_upper)


def _shift_rows(x, k):
    if k == 0:
        return x
    n = x.shape[0]
    rows = lax.broadcasted_iota(jnp.int32, x.shape, 0)
    y = pltpu.roll(x, k % n, axis=0)
    if k > 0:
        return jnp.where(rows >= k, y, 0.0)
    return jnp.where(rows < n + k, y, 0.0)


def _dwconv(x, w, b):
    taps = w.shape[0]
    pad = (taps - 1) // 2
    acc = x * w[pad:pad + 1, :] + b
    for j in range(taps):
        if j != pad:
            acc = acc + _shift_rows(x, pad - j) * w[j:j + 1, :]
    return acc


CONV_HALO = 8


def _conv_blocks(src_ref, pad_ref, w, b, rb, emit):
    seq = src_ref.shape[0]
    taps = w.shape[0]
    pad = (taps - 1) // 2
    assert pad < CONV_HALO
    zeros = jnp.zeros((CONV_HALO, pad_ref.shape[1]), F32)
    pad_ref[0:CONV_HALO, :] = zeros
    pad_ref[CONV_HALO + seq:CONV_HALO + seq + CONV_HALO, :] = zeros
    pad_ref[CONV_HALO:CONV_HALO + seq, :] = src_ref[...].astype(F32)
    n = rb + 2 * CONV_HALO
    for r0 in range(0, seq, rb):
        win = pad_ref[r0:r0 + n, :]
        acc = b
        for j in range(taps):
            shift = pad - j
            moved = win if shift == 0 else pltpu.roll(win, shift % n, axis=0)
            acc = acc + moved[CONV_HALO:CONV_HALO + rb, :] * w[j:j + 1, :]
        emit(r0, acc)


def _rms(x):
    return x * lax.rsqrt(jnp.mean(x * x, axis=-1, keepdims=True) + EPS)


def _inproj_kernel(*refs, tn, with_gates):
    if with_gates:
        x_ref, g_ref, w_ref, wg_ref, o_ref, og_ref = refs
    else:
        x_ref, g_ref, w_ref, o_ref = refs
    xn = (_rms(x_ref[...]) * g_ref[...]).astype(BF16)
    for c0 in range(0, w_ref.shape[1], tn):
        o_ref[:, c0:c0 + tn] = _dot(xn, w_ref[:, c0:c0 + tn]).astype(o_ref.dtype)
    if with_gates:
        og_ref[...] = _dot(xn, wg_ref[...])


def _inproj(x2d, g, w, w_gate, tn):
    m, k = x2d.shape
    n = w.shape[1]
    tm = min(INPROJ_TM, m)
    with_gates = w_gate is not None
    in_specs = [pl.BlockSpec((tm, k), lambda i: (i, 0)),
                pl.BlockSpec((1, k), lambda i: (0, 0)),
                pl.BlockSpec((k, n), lambda i: (0, 0), pipeline_mode=pl.Buffered(1))]
    out_specs = [pl.BlockSpec((tm, n), lambda i: (i, 0))]
    out_shape = [jax.ShapeDtypeStruct((m, n), BF16)]
    args = [x2d, g.reshape(1, k).astype(F32), w]
    if with_gates:
        in_specs.append(pl.BlockSpec((k, LANES), lambda i: (0, 0), pipeline_mode=pl.Buffered(1)))
        out_specs.append(pl.BlockSpec((tm, LANES), lambda i: (i, 0)))
        out_shape.append(jax.ShapeDtypeStruct((m, LANES), F32))
        args.append(w_gate)
    outs = pl.pallas_call(
        functools.partial(_inproj_kernel, tn=tn, with_gates=with_gates),
        grid=(m // tm,),
        in_specs=in_specs,
        out_specs=out_specs,
        out_shape=out_shape,
        compiler_params=_params("parallel"),
        name="inproj",
    )(*args)
    return (outs[0], outs[1]) if with_gates else (outs[0], None)


def _outproj_kernel(y_ref, w_ref, r_ref, g_ref, o_ref, *, final):
    acc = _dot(y_ref[...], w_ref[...]) + r_ref[...]
    if final:
        acc = _rms(acc) * g_ref[...]
    o_ref[...] = acc


def _outproj(y2d, w, res2d, final_g=None):
    m, k = y2d.shape
    n = w.shape[1]
    tm = min(512, m)
    final = final_g is not None
    g = (final_g if final else jnp.ones((n,), F32)).reshape(1, n).astype(F32)
    return pl.pallas_call(
        functools.partial(_outproj_kernel, final=final),
        grid=(m // tm,),
        in_specs=[pl.BlockSpec((tm, k), lambda i: (i, 0)),
                  pl.BlockSpec((k, n), lambda i: (0, 0)),
                  pl.BlockSpec((tm, n), lambda i: (i, 0)),
                  pl.BlockSpec((1, n), lambda i: (0, 0))],
        out_specs=pl.BlockSpec((tm, n), lambda i: (i, 0)),
        out_shape=jax.ShapeDtypeStruct((m, n), F32),
        compiler_params=_params("parallel"),
        name="outproj",
    )(y2d, w.astype(BF16), res2d, g)


def _pad_cols(w, n):
    return jnp.pad(w, ((0, 0), (0, n - w.shape[1])))


def _ssd_kernel(z_ref, x_ref, b_ref, c_ref, dtc_ref, dtr_ref,
                cwx_ref, cwb_ref, cwc_ref, cbx_ref, cbb_ref, cbc_ref,
                biasr_ref, biasc_ref, alogr_ref, alogc_ref, dskip_ref, gnorm_ref,
                o_ref, xs_ref, cs_ref, bt_ref, y_ref, s_ref, u_ref, e1_ref, dec_ref, padx_ref, padn_ref):
    q = SSD_CHUNK
    seq = x_ref.shape[0]
    nc = seq // q
    hpg = SSD_HPG
    assert hpg * SSD_HEAD_DIM == 2 * LANES
    (lower, upper), (t_lower, t_upper) = _tri_consts(q)
    lane_head = lax.shift_right_logical(
        lax.broadcasted_iota(jnp.int32, (1, SSD_GW), 1), int(math.log2(SSD_HEAD_DIM)))
    first_head = lax.broadcasted_iota(jnp.int32, (1, LANES), 1) < SSD_HEAD_DIM
    fwd_lane = lax.broadcasted_iota(jnp.int32, (1, 2 * hpg), 1) < hpg
    fwd_row = lax.broadcasted_iota(jnp.int32, (2 * hpg, 1), 0) < hpg

    def emit_x(r0, blk):
        xs_ref[r0:r0 + q, :] = _silu(blk)

    def emit_c(r0, blk):
        cs_ref[r0:r0 + q, :] = _silu(blk).astype(BF16)

    def emit_b(r0, blk):
        bt_ref[r0 // q] = _silu(blk).T.astype(BF16)

    _conv_blocks(x_ref, padx_ref, cwx_ref[...], cbx_ref[...], q, emit_x)
    _conv_blocks(c_ref, padn_ref, cwc_ref[...], cbc_ref[...], q, emit_c)
    _conv_blocks(b_ref, padn_ref, cwb_ref[...], cbb_ref[...], q, emit_b)

    bias_row = biasr_ref[...]
    bias_col = biasc_ref[...]
    a_row = -jnp.exp(alogr_ref[...])
    a_col = -jnp.exp(alogc_ref[...])
    dskip = dskip_ref[...]

    def rep(cols, j):
        return jnp.broadcast_to(cols[:, j:j + 1], (cols.shape[0], LANES))

    sel_dt = jnp.where(
        lax.broadcasted_iota(jnp.int32, (2 * hpg, 2 * SSD_GW), 0)
        == lax.shift_right_logical(lax.broadcasted_iota(jnp.int32, (2 * hpg, 2 * SSD_GW), 1),
                                   int(math.log2(SSD_HEAD_DIM))), 1.0, 0.0).astype(BF16)

    unroll = 2 if nc % 2 == 0 else 1

    def local_body(i, carry):
        chunks = [i * unroll + j for j in range(unroll)]
        rows = [pl.ds(pl.multiple_of(c * q, q), q) for c in chunks]
        us = range(unroll)
        dt_col = [_softplus(dtc_ref[r, :] + bias_row) for r in rows]
        dt_row = [_softplus(dtr_ref[c] + bias_col) for c in chunks]
        la_col = [dt * a_row for dt in dt_col]
        la_row = [dt * a_col for dt in dt_row]
        pre_col = [_tri_dot(t_lower, la) for la in la_col]
        pre_row = [_dot_tri(la, t_upper) for la in la_row]
        cbs = [_dot(cs_ref[rows[j], :], bt_ref[chunks[j]]) for j in us]
        dt_exp = [_dot(dt.astype(BF16), sel_dt) for dt in dt_col]
        ys, xqs, cum_exps = [], [], []
        for j in us:
            cum_col = jnp.where(fwd_lane, pre_col[j], pre_col[j][q - 1:q, :] - pre_col[j] + la_col[j])
            cum_row = jnp.where(fwd_row, pre_row[j], pre_row[j][:, q - 1:q] - pre_row[j] + la_row[j])
            reps = [rep(cum_col, r) for r in range(2 * hpg)]
            ms = []
            for r in range(hpg):
                rb = hpg + r
                mf = jnp.exp(jnp.where(lower, reps[r] - cum_row[r:r + 1, :], _NEG_INF)) * dt_row[j][r:r + 1, :]
                mb = jnp.exp(jnp.where(upper, reps[rb] - cum_row[rb:rb + 1, :], _NEG_INF)) * dt_row[j][rb:rb + 1, :]
                ms.append((cbs[j] * (mf + mb)).astype(BF16))
            xq = xs_ref[rows[j], :]
            xb = xq.astype(BF16)
            xbd = jnp.concatenate(
                [jnp.where(lane_head == r, xb, jnp.zeros_like(xb)) for r in range(hpg)], axis=0)
            ys.append(_dot(jnp.concatenate(ms, axis=1), xbd))
            xqs.append(xq)
            cum_exps.append([jnp.concatenate(
                [jnp.where(first_head, reps[lo], reps[lo + 1]),
                 jnp.where(first_head, reps[lo + 2], reps[lo + 3])], axis=1) for lo in (0, hpg)])
        for j in us:
            y_ref[rows[j], :] = ys[j] + xqs[j] * dskip
            for d in range(2):
                cum_exp = cum_exps[j][d]
                tot = cum_exp[q - 1:q, :] if d == 0 else cum_exp[0:1, :]
                w = dt_exp[j][:, d * SSD_GW:(d + 1) * SSD_GW] * jnp.exp(tot - cum_exp)
                u_ref[d, chunks[j]] = _dot(bt_ref[chunks[j]], (xqs[j] * w).astype(BF16))
                e1_ref[d, rows[j], :] = jnp.exp(cum_exp).astype(BF16)
                dec_ref[d, chunks[j]] = jnp.exp(tot)
        return carry

    lax.fori_loop(0, nc // unroll, local_body, 0)

    s_ref[...] = jnp.zeros_like(s_ref)

    def serial(i, carry):
        chunks = (i, nc - 1 - i)
        rows = [pl.ds(pl.multiple_of(c * q, q), q) for c in chunks]
        states = [s_ref[d] for d in range(2)]
        inter = [_dot(cs_ref[rows[d], :], states[d].astype(BF16)) for d in range(2)]
        for d in range(2):
            s_ref[d] = states[d] * dec_ref[d, chunks[d]] + u_ref[d, chunks[d]]
        for d in range(2):
            y_ref[rows[d], :] += inter[d] * e1_ref[d, rows[d], :].astype(F32)
        return carry

    lax.fori_loop(0, nc, serial, 0)

    y = y_ref[...] * _silu(z_ref[...].astype(F32))
    o_ref[...] = (_rms(y) * gnorm_ref[...]).astype(o_ref.dtype)


def _ssd_layer(h, norm, w_in, conv_w, conv_b, dt_bias, a_log, d_skip, gnorm, w_out, final_g):
    bsz, seq, _ = h.shape
    g, hpg, q = SSD_GROUPS, SSD_HPG, SSD_CHUNK
    nc = seq // q
    h2 = h.reshape(bsz * seq, D_MODEL)
    n_main = 2 * D_INNER + 2 * SSD_BC
    pm, gates = _inproj(h2, norm, w_in[:, :n_main].astype(BF16),
                        _pad_cols(w_in[:, n_main:], LANES).astype(BF16), INPROJ_TN)
    pm = pm.reshape(bsz, seq, n_main)
    dt = gates[:, :2 * SSD_HEADS].reshape(bsz, seq, 2, g, hpg)
    dt_col = dt.transpose(0, 3, 1, 2, 4).reshape(bsz, g, seq, 2 * hpg)
    dt_row = dt.reshape(bsz, nc, q, 2, g, hpg).transpose(0, 4, 1, 3, 5, 2).reshape(bsz, g, nc, 2 * hpg, q)

    def per_group(p):
        return p.astype(F32).reshape(2, g, hpg).transpose(1, 0, 2).reshape(g, 2 * hpg)

    bias, alog = per_group(dt_bias), per_group(a_log)
    dskip = jnp.repeat(d_skip.astype(F32).reshape(g, hpg), SSD_HEAD_DIM, axis=1).reshape(g, 1, SSD_GW)
    cw = conv_w.astype(F32)
    cb = conv_b.astype(F32).reshape(1, -1)
    taps = cw.shape[0]
    xo = D_INNER // SSD_GW
    bo = 2 * D_INNER // SSD_STATE
    co = bo + SSD_GROUPS
    cbo = D_INNER // SSD_STATE
    cco = cbo + SSD_GROUPS

    y = pl.pallas_call(
        _ssd_kernel,
        grid=(bsz, g),
        in_specs=[
            pl.BlockSpec((None, seq, SSD_GW), lambda b, j: (b, 0, j)),
            pl.BlockSpec((None, seq, SSD_GW), lambda b, j: (b, 0, xo + j)),
            pl.BlockSpec((None, seq, SSD_STATE), lambda b, j: (b, 0, bo + j)),
            pl.BlockSpec((None, seq, SSD_STATE), lambda b, j: (b, 0, co + j)),
            pl.BlockSpec((None, None, seq, 2 * hpg), lambda b, j: (b, j, 0, 0)),
            pl.BlockSpec((None, None, nc, 2 * hpg, q), lambda b, j: (b, j, 0, 0, 0)),
            pl.BlockSpec((taps, SSD_GW), lambda b, j: (0, j)),
            pl.BlockSpec((taps, SSD_STATE), lambda b, j: (0, cbo + j)),
            pl.BlockSpec((taps, SSD_STATE), lambda b, j: (0, cco + j)),
            pl.BlockSpec((1, SSD_GW), lambda b, j: (0, j)),
            pl.BlockSpec((1, SSD_STATE), lambda b, j: (0, cbo + j)),
            pl.BlockSpec((1, SSD_STATE), lambda b, j: (0, cco + j)),
            pl.BlockSpec((None, 1, 2 * hpg), lambda b, j: (j, 0, 0)),
            pl.BlockSpec((None, 2 * hpg, 1), lambda b, j: (j, 0, 0)),
            pl.BlockSpec((None, 1, 2 * hpg), lambda b, j: (j, 0, 0)),
            pl.BlockSpec((None, 2 * hpg, 1), lambda b, j: (j, 0, 0)),
            pl.BlockSpec((None, 1, SSD_GW), lambda b, j: (j, 0, 0)),
            pl.BlockSpec((None, 1, SSD_GW), lambda b, j: (j, 0, 0)),
        ],
        out_specs=pl.BlockSpec((None, seq, SSD_GW), lambda b, j: (b, 0, j)),
        out_shape=jax.ShapeDtypeStruct((bsz, seq, D_INNER), BF16),
        scratch_shapes=[pltpu.VMEM((seq, SSD_GW), F32),
                        pltpu.VMEM((seq, SSD_STATE), BF16),
                        pltpu.VMEM((nc, SSD_STATE, q), BF16),
                        pltpu.VMEM((seq, SSD_GW), F32),
                        pltpu.VMEM((2, SSD_STATE, SSD_GW), F32),
                        pltpu.VMEM((2, nc, SSD_STATE, SSD_GW), F32),
                        pltpu.VMEM((2, seq, SSD_GW), BF16),
                        pltpu.VMEM((2, nc, 1, SSD_GW), F32),
                        pltpu.VMEM((seq + 2 * CONV_HALO, SSD_GW), F32),
                        pltpu.VMEM((seq + 2 * CONV_HALO, SSD_STATE), F32)],
        compiler_params=_params("parallel", "parallel"),
        name="ssd_mixer",
    )(pm, pm, pm, pm, dt_col, dt_row, cw, cw, cw, cb, cb, cb,
      bias.reshape(g, 1, 2 * hpg), bias.reshape(g, 2 * hpg, 1),
      alog.reshape(g, 1, 2 * hpg), alog.reshape(g, 2 * hpg, 1),
      dskip, gnorm.astype(F32).reshape(g, 1, SSD_GW))
    out = _outproj(y.reshape(bsz * seq, D_INNER), w_out, h2, final_g)
    return out.reshape(bsz, seq, D_MODEL)


def _gla_kernel(q_ref, k_ref, v_ref, z_ref, gl_ref, wg_ref, bg_ref, onorm_ref,
                o_ref, lg_ref, acc_ref, s_ref, qg_ref, u_ref, dec_ref):
    q = GLA_CHUNK
    seq = q_ref.shape[0]
    nc = seq // q
    masks, tris = _tri_consts(q)
    gl = gl_ref[...]
    for d in range(2):
        lg_ref[d] = _log_sigmoid(_dot_tri(gl, wg_ref[d]) + bg_ref[d]) * (1.0 / GLA_NORMALIZER)

    unroll = 2 if nc % 2 == 0 else 1

    def local_body(i, carry):
        chunks = [i * unroll + j for j in range(unroll)]
        rows = [pl.ds(pl.multiple_of(c * q, q), q) for c in chunks]
        pairs = [(j, d) for j in range(unroll) for d in range(2)]
        cums = {(j, d): _tri_dot(tris[d], lg_ref[d, rows[j], :]) for j, d in pairs}
        qgs, kgs, kds = {}, {}, {}
        for j, d in pairs:
            cum = cums[j, d]
            qc = q_ref[rows[j], :].astype(F32) * (GLA_DK ** -0.5)
            kc = k_ref[rows[j], :].astype(F32)
            tot = cum[q - 1:q, :] if d == 0 else cum[0:1, :]
            qgs[j, d] = (qc * jnp.exp(cum)).astype(BF16)
            kgs[j, d] = (kc * jnp.exp(-cum)).astype(BF16)
            kds[j, d] = (kc * jnp.exp(tot - cum)).astype(BF16)
            qg_ref[d, rows[j], :] = qgs[j, d]
            dec_ref[d, chunks[j]] = jnp.exp(tot)
        atts = {p: _dot_nt(qgs[p], kgs[p]) for p in pairs}
        for j, d in pairs:
            u_ref[d, chunks[j]] = _dot_tn(v_ref[rows[j], :], kds[j, d]).astype(BF16)
        atts = {(j, d): jnp.where(masks[d], atts[j, d], 0.0).astype(BF16) for j, d in pairs}
        outs = {(j, d): _dot(atts[j, d], v_ref[rows[j], :]) for j, d in pairs}
        for j in range(unroll):
            acc_ref[rows[j], :] = outs[j, 0] + outs[j, 1]
        return carry

    lax.fori_loop(0, nc // unroll, local_body, 0)

    s_ref[...] = jnp.zeros_like(s_ref)

    def serial(i, carry):
        chunks = (i, nc - 1 - i)
        rows = [pl.ds(pl.multiple_of(c * q, q), q) for c in chunks]
        states = [s_ref[d] for d in range(2)]
        inter = [_dot_nt(qg_ref[d, rows[d], :], states[d].astype(BF16)) for d in range(2)]
        for d in range(2):
            s_ref[d] = states[d] * dec_ref[d, chunks[d]] + u_ref[d, chunks[d]].astype(F32)
        for d in range(2):
            acc_ref[rows[d], :] += inter[d]
        return carry

    lax.fori_loop(0, nc, serial, 0)

    out = _rms(acc_ref[...]) * onorm_ref[...]
    o_ref[...] = (out * _silu(z_ref[...].astype(F32))).astype(o_ref.dtype)


def _gla_layer(h, norm, w_in, w_gate, b_gate, onorm, w_out, final_g):
    bsz, seq, _ = h.shape
    nh = GLA_HEADS
    h2 = h.reshape(bsz * seq, D_MODEL)
    n_main = 2 * GLA_QK + 2 * D_INNER
    pm, gates = _inproj(h2, norm, w_in[:, :n_main].astype(BF16),
                        _pad_cols(w_in[:, n_main:], LANES).astype(BF16), INPROJ_TN)
    pm = pm.reshape(bsz, seq, n_main)
    gates = gates.reshape(bsz, seq, LANES)
    wg = jnp.zeros((2, LANES, GLA_QK), F32)
    for d in range(2):
        wg = wg.at[d, d * GLA_RANK:(d + 1) * GLA_RANK, :].set(w_gate[d].astype(F32))
    vo = 2 * GLA_QK // GLA_DV
    zo = vo + nh
    y = pl.pallas_call(
        _gla_kernel,
        grid=(bsz, nh),
        in_specs=[
            pl.BlockSpec((None, seq, GLA_DK), lambda b, j: (b, 0, j)),
            pl.BlockSpec((None, seq, GLA_DK), lambda b, j: (b, 0, nh + j)),
            pl.BlockSpec((None, seq, GLA_DV), lambda b, j: (b, 0, vo + j)),
            pl.BlockSpec((None, seq, GLA_DV), lambda b, j: (b, 0, zo + j)),
            pl.BlockSpec((None, seq, LANES), lambda b, j: (b, 0, 0)),
            pl.BlockSpec((2, LANES, GLA_DK), lambda b, j: (0, 0, j)),
            pl.BlockSpec((2, 1, GLA_DK), lambda b, j: (0, 0, j)),
            pl.BlockSpec((1, GLA_DV), lambda b, j: (0, 0)),
        ],
        out_specs=pl.BlockSpec((None, seq, GLA_DV), lambda b, j: (b, 0, j)),
        out_shape=jax.ShapeDtypeStruct((bsz, seq, D_INNER), BF16),
        scratch_shapes=[pltpu.VMEM((2, seq, GLA_DK), F32),
                        pltpu.VMEM((seq, GLA_DV), F32),
                        pltpu.VMEM((2, GLA_DV, GLA_DK), F32),
                        pltpu.VMEM((2, seq, GLA_DK), BF16),
                        pltpu.VMEM((2, seq // GLA_CHUNK, GLA_DV, GLA_DK), BF16),
                        pltpu.VMEM((2, seq // GLA_CHUNK, 1, GLA_DK), F32)],
        compiler_params=_params("parallel", "parallel"),
        name="gla_mixer",
    )(pm, pm, pm, pm, gates, wg.astype(BF16), b_gate.astype(F32).reshape(2, 1, GLA_QK),
      onorm.astype(F32).reshape(1, GLA_DV))
    out = _outproj(y.reshape(bsz * seq, D_INNER), w_out, h2, final_g)
    return out.reshape(bsz, seq, D_MODEL)


def _hy_mlp_kernel(f_ref, w0_ref, b0_ref, w1_ref, b1_ref, w2_ref, b2_ref, fr_ref, o_ref):
    fr = fr_ref[...]
    hid = jnp.sin(fr[0:1, :] * (_dot_f32(f_ref[...], w0_ref[...]) + b0_ref[...]))
    hid = jnp.sin(fr[1:2, :] * (_dot_f32(hid, w1_ref[...]) + b1_ref[...]))
    hid = jnp.sin(fr[2:3, :] * (_dot_f32(hid, w2_ref[...]) + b2_ref[...]))
    o_ref[...] = hid


def _hy_spec_kernel(hid_ref, wf_ref, wb_ref, t_ref, dl_ref, cos_ref, sin_ref, cosr_ref, sinr_ref,
                    kre_ref, ks_ref, p_ref):
    lb = hid_ref.shape[0] // 2

    @pl.when(pl.program_id(2) == 0)
    def _():
        env = jnp.exp(-t_ref[...] * dl_ref[...])
        hf = _dot_f32(hid_ref[...], wf_ref[...]) * env
        hb = _dot_f32(hid_ref[...], wb_ref[...]) * env
        first = lax.broadcasted_iota(jnp.int32, (lb, hf.shape[1]), 0) == 0
        hf_lo0 = jnp.where(first, 0.0, hf[0:lb])
        hb_lo0 = jnp.where(first, 0.0, hb[0:lb])
        p_ref[0] = (hf[0:lb] + hb_lo0).astype(BF16)
        p_ref[1] = (hf[0:lb] - hb_lo0).astype(BF16)
        p_ref[2] = hf[lb:].astype(BF16)
        p_ref[3] = hf_lo0.astype(BF16)
        p_ref[4] = hb[lb:].astype(BF16)
        p_ref[5] = hb_lo0.astype(BF16)

    scale = 1.0 / lb
    cos, sin, cosr, sinr = cos_ref[...], sin_ref[...], cosr_ref[...], sinr_ref[...]
    dt = kre_ref.dtype
    kre_ref[0] = (_dot(cos, p_ref[0]) * scale).astype(dt)
    ks_ref[0] = (_dot(sin, p_ref[1]) * scale).astype(dt)
    kre_ref[1] = ((_dot(cos, p_ref[2]) + _dot(cosr, p_ref[3])) * scale).astype(dt)
    ks_ref[1] = ((_dot(sin, p_ref[2]) + _dot(sinr, p_ref[3])) * scale).astype(dt)
    kre_ref[2] = ((_dot(cos, p_ref[4]) + _dot(cosr, p_ref[5])) * scale).astype(dt)
    ks_ref[2] = ((_dot(sin, p_ref[4]) + _dot(sinr, p_ref[5])) * (-scale)).astype(dt)


def _hy_conv_kernel(v_ref, x1_ref, x2_ref, z_ref, cwv_ref, cw1_ref, cw2_ref,
                    cbv_ref, cb1_ref, cb2_ref, kre_ref, ks_ref, d_ref, w_ref, o_ref,
                    y_ref, yb_ref, zf_ref, g_ref, *, fb):
    seq, tc = v_ref.shape
    lb = seq // 2
    nblk = lb // fb
    y_ref[...] = _dwconv(v_ref[...].astype(F32), cwv_ref[...], cbv_ref[...])
    gates = ((x1_ref, cw1_ref, cb1_ref), (x2_ref, cw2_ref, cb2_ref))
    for o, (x_ref, cw_ref, cb_ref) in enumerate(gates):
        yb_ref[:, 0:tc] = y_ref[0:lb, :].astype(BF16)
        yb_ref[:, tc:2 * tc] = y_ref[lb:seq, :].astype(BF16)
        g_ref[...] = _dwconv(x_ref[...].astype(F32), cw_ref[...], cb_ref[...])

        def forward(k, carry, o=o):
            f0 = pl.multiple_of(k * fb, fb)
            yb = yb_ref[...]
            yre = _dot(w_ref[pl.ds(f0, fb), :], yb)
            yim = _dot(w_ref[pl.ds(lb + f0, fb), :], yb)
            re = [yre[:, 0:tc], yre[:, tc:2 * tc]]
            im = [yim[:, 0:tc], yim[:, tc:2 * tc]]
            kre = [kre_ref[o, n, pl.ds(f0, fb), :].astype(F32) for n in range(3)]
            ks = [ks_ref[o, n, pl.ds(f0, fb), :].astype(F32) for n in range(3)]
            for i, (n0, n1) in enumerate(((0, 2), (1, 0))):
                zre = re[0] * kre[n0] - im[0] * ks[n0] + re[1] * kre[n1] - im[1] * ks[n1]
                zim = im[0] * kre[n0] + re[0] * ks[n0] + im[1] * kre[n1] + re[1] * ks[n1]
                zf_ref[pl.ds(f0, fb), i * tc:(i + 1) * tc] = zre.astype(BF16)
                zf_ref[pl.ds(lb + f0, fb), i * tc:(i + 1) * tc] = zim.astype(BF16)
            return carry

        lax.fori_loop(0, nblk, forward, 0)

        def inverse(k, carry, o=o):
            t0 = pl.multiple_of(k * fb, fb)
            conv = (_dot(w_ref[pl.ds(t0, fb), :], zf_ref[0:lb, :])
                    + _dot(w_ref[pl.ds(lb + t0, fb), :], zf_ref[lb:2 * lb, :]))
            for i in range(2):
                rows = pl.ds(i * lb + t0, fb)
                y_ref[rows, :] = g_ref[rows, :] * (conv[:, i * tc:(i + 1) * tc]
                                                   + y_ref[rows, :] * d_ref[o:o + 1, :])
            return carry

        lax.fori_loop(0, nblk, inverse, 0)
    o_ref[...] = (y_ref[...] * _silu(z_ref[...].astype(F32))).astype(o_ref.dtype)


def _hyena_layer(h, norm, w_in, conv_w, conv_b, ffn_w_in, ffn_b_in, ffn_w_hid, ffn_b_hid,
                 ffn_freq, ffn_w_out, d_bias, w_out, final_g):
    bsz, seq, _ = h.shape
    c = D_INNER
    h2 = h.reshape(bsz * seq, D_MODEL)
    pm = _inproj(h2, norm, w_in.astype(BF16), None, INPROJ_TN)[0].reshape(bsz, seq, 4 * c)

    t = jnp.linspace(0.0, 1.0, seq, dtype=F32)[:, None]
    pos = jnp.arange(seq, dtype=F32)[:, None]
    bands = jnp.linspace(1e-4, HY_BANDS - 1, HY_BANDS, dtype=F32)[None]
    ang = (2.0 * math.pi / seq) * pos * bands
    feats = jnp.concatenate([t, jnp.cos(ang), -jnp.sin(ang)], axis=-1)
    feats = _pad_cols(feats, LANES)
    w0 = jnp.pad(ffn_w_in.astype(F32), ((0, LANES - HY_EMB), (0, 0)))
    hid = pl.pallas_call(
        _hy_mlp_kernel,
        out_shape=jax.ShapeDtypeStruct((seq, HY_FFN), F32),
        name="hyena_filter_mlp",
    )(feats, w0, ffn_b_in.astype(F32).reshape(1, HY_FFN),
      ffn_w_hid[0].astype(F32), ffn_b_hid[0].astype(F32).reshape(1, HY_FFN),
      ffn_w_hid[1].astype(F32), ffn_b_hid[1].astype(F32).reshape(1, HY_FFN),
      ffn_freq.astype(F32))

    max_decay = math.log(HY_TARGET) / HY_FAST_PCT
    min_decay = math.log(HY_TARGET) / HY_SLOW_PCT
    deltas = jnp.abs(jnp.linspace(min_decay, max_decay, c, dtype=F32)).reshape(1, c)

    lb = seq // 2
    fi = jnp.arange(lb, dtype=jnp.int32)
    n4 = 8 * lb

    def table(num, fn):
        return fn((2.0 * math.pi / n4) * (num % n4).astype(F32)).astype(BF16)

    num1 = (2 * fi[:, None] + 1) * (2 * fi[None, :])
    numr = (2 * fi[:, None] + 1) * (2 * (fi[None, :] - lb))
    num2 = (2 * fi[:, None] + 1) * (2 * fi[None, :] + 1)
    cos1, sin1 = table(num1, jnp.cos), table(num1, jnp.sin)
    cosr, sinr = table(numr, jnp.cos), table(numr, jnp.sin)
    w2 = jnp.concatenate([table(num2, jnp.cos), table(num2, jnp.sin)], axis=0)

    tc = min(512, c)
    fb = min(512, lb)
    nct = c // tc
    wo = ffn_w_out.astype(F32)
    tab_spec = pl.BlockSpec((fb, lb), lambda o, j, k: (k, 0))
    kre, ks = pl.pallas_call(
        _hy_spec_kernel,
        grid=(HY_ORDER, nct, lb // fb),
        in_specs=[
            pl.BlockSpec((seq, HY_FFN), lambda o, j, k: (0, 0)),
            pl.BlockSpec((HY_FFN, tc), lambda o, j, k: (0, o * nct + j)),
            pl.BlockSpec((HY_FFN, tc), lambda o, j, k: (0, (HY_ORDER + o) * nct + j)),
            pl.BlockSpec((seq, 1), lambda o, j, k: (0, 0)),
            pl.BlockSpec((1, tc), lambda o, j, k: (0, j)),
            tab_spec, tab_spec, tab_spec, tab_spec,
        ],
        out_specs=[pl.BlockSpec((None, 3, fb, tc), lambda o, j, k: (o, 0, k, j)),
                   pl.BlockSpec((None, 3, fb, tc), lambda o, j, k: (o, 0, k, j))],
        out_shape=[jax.ShapeDtypeStruct((HY_ORDER, 3, lb, c), BF16)] * 2,
        scratch_shapes=[pltpu.VMEM((6, lb, tc), BF16)],
        compiler_params=_params("parallel", "parallel", "arbitrary"),
        name="hyena_filter_spectrum",
    )(hid, wo, wo, t, deltas, cos1, sin1, cosr, sinr)

    tcc = min(256, c)
    ncc = c // tcc
    cw = conv_w.astype(F32)
    cb = conv_b.astype(F32).reshape(1, -1)
    taps = cw.shape[0]
    y = pl.pallas_call(
        functools.partial(_hy_conv_kernel, fb=fb),
        grid=(ncc, bsz),
        in_specs=[
            pl.BlockSpec((None, seq, tcc), lambda j, b: (b, 0, j)),
            pl.BlockSpec((None, seq, tcc), lambda j, b: (b, 0, ncc + j)),
            pl.BlockSpec((None, seq, tcc), lambda j, b: (b, 0, 2 * ncc + j)),
            pl.BlockSpec((None, seq, tcc), lambda j, b: (b, 0, 3 * ncc + j)),
            pl.BlockSpec((taps, tcc), lambda j, b: (0, j)),
            pl.BlockSpec((taps, tcc), lambda j, b: (0, ncc + j)),
            pl.BlockSpec((taps, tcc), lambda j, b: (0, 2 * ncc + j)),
            pl.BlockSpec((1, tcc), lambda j, b: (0, j)),
            pl.BlockSpec((1, tcc), lambda j, b: (0, ncc + j)),
            pl.BlockSpec((1, tcc), lambda j, b: (0, 2 * ncc + j)),
            pl.BlockSpec((HY_ORDER, 3, lb, tcc), lambda j, b: (0, 0, 0, j), pipeline_mode=pl.Buffered(1)),
            pl.BlockSpec((HY_ORDER, 3, lb, tcc), lambda j, b: (0, 0, 0, j), pipeline_mode=pl.Buffered(1)),
            pl.BlockSpec((HY_ORDER, tcc), lambda j, b: (0, j)),
            pl.BlockSpec((2 * lb, lb), lambda j, b: (0, 0), pipeline_mode=pl.Buffered(1)),
        ],
        out_specs=pl.BlockSpec((None, seq, tcc), lambda j, b: (b, 0, j)),
        out_shape=jax.ShapeDtypeStruct((bsz, seq, c), BF16),
        scratch_shapes=[pltpu.VMEM((seq, tcc), F32),
                        pltpu.VMEM((lb, 2 * tcc), BF16),
                        pltpu.VMEM((2 * lb, 2 * tcc), BF16),
                        pltpu.VMEM((seq, tcc), F32)],
        compiler_params=_params("parallel", "parallel"),
        name="hyena_mixer",
    )(pm, pm, pm, pm, cw, cw, cw, cb, cb, cb, kre, ks, d_bias.astype(F32), w2)
    out = _outproj(y.reshape(bsz * seq, c), w_out, h2, final_g)
    return out.reshape(bsz, seq, D_MODEL)


def _mlstm_kernel(xm_ref, z_ref, og_ref, gc_ref, gr_ref, cw_ref, cb_ref, wq_ref, wk_ref, wv_ref,
                  gbr_ref, gbc_ref, skip_ref, onorm_ref,
                  o_ref, q_ref, k_ref, v_ref, ch_ref, acc_ref, c_ref, nl_ref, st_ref, kt_ref, str_ref, cmat_ref,
                  inc_ref):
    q = ML_CHUNK
    seq = xm_ref.shape[0]
    nc = seq // q
    masks, tris = _tri_consts(q)

    for j in range(ML_DH // LANES):
        sl = slice(j * LANES, (j + 1) * LANES)
        ch_ref[:, sl] = _silu(_dwconv(xm_ref[:, sl].astype(F32), cw_ref[:, sl], cb_ref[:, sl])).astype(BF16)
    ch = ch_ref[...]
    q_ref[...] = _dot(ch, wq_ref[...]).astype(BF16)
    kf = _dot(ch, wk_ref[...]) * (ML_DK ** -0.5)
    k_ref[...] = kf.astype(BF16)
    for c in range(nc):
        kt_ref[c] = kf[c * q:(c + 1) * q, :].T.astype(BF16)
    v_ref[...] =_dot(xm_ref[...], wv_ref[...]).astype(BF16)

    gb_row = gbr_ref[...]
    gb_col = gbc_ref[...]

    (t_lower, t_upper) = tris
    stat_lane = lax.broadcasted_iota(jnp.int32, (1, 8), 1)

    unroll = 2 if nc % 2 == 0 else 1

    def local_body(i, carry):
        chunks = [i * unroll + j for j in range(unroll)]
        rows = [pl.ds(pl.multiple_of(c * q, q), q) for c in chunks]
        pairs = [(j, d) for j in range(unroll) for d in range(2)]
        g_col = [gc_ref[r, :] + gb_row for r in rows]
        g_row = [gr_ref[c] + gb_col for c in chunks]
        lf_col = [_log_sigmoid(g) for g in g_col]
        lf_row = [_log_sigmoid(g) for g in g_row]
        tri_c, tri_r = (t_lower, t_upper), (t_upper, t_lower)
        b_cols = {(j, d): _tri_dot(tri_c[d], lf_col[j]) for j, d in pairs}
        b_rows = {(j, d): _dot_tri(lf_row[j], tri_r[d]) for j, d in pairs}
        qk = [_dot_nt(q_ref[r, :], k_ref[r, :]) for r in rows]
        ss = {}
        for j, d in pairs:
            i_col = g_col[j][:, 2 * d:2 * d + 1]
            i_row = g_row[j][2 * d:2 * d + 1, :]
            b_col = b_cols[j, d][:, 2 * d + 1:2 * d + 2]
            b_row = b_rows[j, d][2 * d + 1:2 * d + 2, :]
            logd = jnp.where(masks[d], b_col - b_row + i_row, _NEG_INF)
            m_loc = jnp.max(logd, axis=1, keepdims=True)
            s = qk[j] * jnp.exp(logd - m_loc)
            ss[j, d] = s.astype(BF16)
            den_loc = jnp.sum(s, axis=1, keepdims=True)
            tot = b_col[q - 1:q, :] if d == 0 else b_col[0:1, :]
            lw_col = tot - b_col + i_col
            lw_max = jnp.max(lw_col, axis=0, keepdims=True)
            stats = (m_loc, den_loc, b_col, lw_col - lw_max, jnp.broadcast_to(lw_max, (q, 1)))
            packed = stats[-1]
            for lane in range(len(stats) - 2, -1, -1):
                packed = jnp.where(stat_lane == lane, stats[lane], packed)
            st_ref[d, rows[j], :] = packed
            str_ref[d, chunks[j]] = jnp.broadcast_to(tot - b_row + i_row - lw_max, (8, q))
        for j, d in pairs:
            nl_ref[d, rows[j], :] = _dot(ss[j, d], v_ref[rows[j], :]).astype(BF16)
        return carry

    lax.fori_loop(0, nc // unroll, local_body, 0)

    c_ref[...] = jnp.zeros_like(c_ref)
    cmat_ref[...] = jnp.zeros_like(cmat_ref)
    acc_ref[...] = jnp.zeros_like(acc_ref)

    def serial(i, carry):
        chunks = (i, nc - 1 - i)
        rows = [pl.ds(pl.multiple_of(c * q, q), q) for c in chunks]
        w_rows = [jnp.exp(str_ref[d, chunks[d]]).astype(BF16) for d in range(2)]
        for d in range(2):
            inc_ref[d] = _dot(kt_ref[chunks[d]] * w_rows[d][0:1, :], v_ref[rows[d], :])
        ksums = [_dot(w_rows[d], k_ref[rows[d], :])[0:1, :] for d in range(2)]
        new, decs, gains = [], [], []
        for d in range(2):
            n_row, m = carry[d]
            st = st_ref[d, rows[d], :]
            m_loc, den_loc, b_col, lw_max = st[:, 0:1], st[:, 1:2], st[:, 2:3], st[0:1, 4:5]
            inter = b_col + m
            m_row = jnp.maximum(m_loc, inter)
            a_loc = jnp.exp(m_loc - m_row)
            a_int = jnp.exp(inter - m_row)
            tot = b_col[q - 1:q, :] if d == 0 else b_col[0:1, :]
            m_new = jnp.maximum(tot + m, lw_max)
            dec = jnp.exp(tot + m - m_new)
            gain = jnp.exp(lw_max - m_new)
            qc = q_ref[rows[d], :]
            den = den_loc * a_loc + jnp.sum(qc.astype(F32) * n_row, axis=1, keepdims=True) * a_int
            scale = 1.0 / jnp.maximum(jnp.abs(den), jnp.exp(-m_row))
            acc_ref[rows[d], :] += (nl_ref[d, rows[d], :].astype(F32) * (a_loc * scale)
                                    + _dot(qc, cmat_ref[d]) * (a_int * scale))
            decs.append(dec)
            gains.append(gain)
            new.append((n_row * dec + ksums[d] * gain, m_new))

        blk = 32
        for d in range(2):
            for r in range(0, ML_DK, blk):
                cnew = c_ref[d, r:r + blk, :] * decs[d] + inc_ref[d, r:r + blk, :] * gains[d]
                c_ref[d, r:r + blk, :] = cnew
                cmat_ref[d, r:r + blk, :] = cnew.astype(BF16)
        return tuple(new)

    init = (jnp.zeros((1, ML_DK), F32), jnp.full((1, 1), _NEG_INF, F32))
    lax.fori_loop(0, nc, serial, (init, init))

    hh = _sigmoid(og_ref[...].astype(F32)) * acc_ref[...]
    hh = _rms(hh) * onorm_ref[...] + skip_ref[...] * ch_ref[...].astype(F32)
    o_ref[...] = (hh * _silu(z_ref[...].astype(F32))).astype(o_ref.dtype)


def _mlstm_layer(h, norm, w_in, conv_w, conv_b, w_q, w_k, w_v, gate_b, skip, onorm, w_out, final_g):
    bsz, seq, _ = h.shape
    nh, q = ML_HEADS, ML_CHUNK
    nc = seq // q
    h2 = h.reshape(bsz * seq, D_MODEL)
    n_main = 3 * D_INNER
    pm, gates = _inproj(h2, norm, w_in[:, :n_main].astype(BF16),
                        _pad_cols(w_in[:, n_main:], LANES).astype(BF16), INPROJ_TN)
    pm = pm.reshape(bsz, seq, n_main)
    gt = gates[:, :4 * nh].reshape(bsz, seq, 2, 2, nh)
    g_col = gt.transpose(0, 4, 1, 2, 3).reshape(bsz, nh, seq, 4)
    g_row = gt.reshape(bsz, nc, q, 2, 2, nh).transpose(0, 5, 1, 3, 4, 2).reshape(bsz, nh, nc, 4, q)
    g_row = jnp.pad(g_row, ((0, 0), (0, 0), (0, 0), (0, 4), (0, 0)))
    gb =gate_b.astype(F32).transpose(2, 0, 1).reshape(nh, 4)
    taps = conv_w.shape[0]
    y = pl.pallas_call(
        _mlstm_kernel,
        grid=(bsz, nh),
        in_specs=[
            pl.BlockSpec((None, seq, ML_DH), lambda b, j: (b, 0, j)),
            pl.BlockSpec((None, seq, ML_DH), lambda b, j: (b, 0, nh + j)),
            pl.BlockSpec((None, seq, ML_DH), lambda b, j: (b, 0, 2 * nh + j)),
            pl.BlockSpec((None, None, seq, 4), lambda b, j: (b, j, 0, 0)),
            pl.BlockSpec((None, None, nc, 8, q), lambda b, j: (b, j, 0, 0, 0)),
            pl.BlockSpec((taps, ML_DH), lambda b, j: (0, j)),
            pl.BlockSpec((1, ML_DH), lambda b, j: (0, j)),
            pl.BlockSpec((None, ML_DH, ML_DK), lambda b, j: (j, 0, 0)),
            pl.BlockSpec((None, ML_DH, ML_DK), lambda b, j: (j, 0, 0)),
            pl.BlockSpec((None, ML_DH, ML_DH), lambda b, j: (j, 0, 0)),
            pl.BlockSpec((None, 1, 4), lambda b, j: (j, 0, 0)),
            pl.BlockSpec((None, 8, 1), lambda b, j: (j, 0, 0)),
            pl.BlockSpec((None, 1, ML_DH), lambda b, j: (j, 0, 0)),
            pl.BlockSpec((1, ML_DH), lambda b, j: (0, 0)),
        ],
        out_specs=pl.BlockSpec((None, seq, ML_DH), lambda b, j: (b, 0, j)),
        out_shape=jax.ShapeDtypeStruct((bsz, seq, D_INNER), BF16),
        scratch_shapes=[pltpu.VMEM((seq, ML_DK), BF16),
                        pltpu.VMEM((seq, ML_DK), BF16),
                        pltpu.VMEM((seq, ML_DH), BF16),
                        pltpu.VMEM((seq, ML_DH), BF16),
                        pltpu.VMEM((seq, ML_DH), F32),
                        pltpu.VMEM((2, ML_DK, ML_DH), F32),
                        pltpu.VMEM((2, seq, ML_DH), BF16),
                        pltpu.VMEM((2, seq, 8), F32),
                        pltpu.VMEM((nc, ML_DK, q), BF16),
                        pltpu.VMEM((2, nc, 8, q), F32),
                        pltpu.VMEM((2, ML_DK, ML_DH), BF16),
                        pltpu.VMEM((2, ML_DK, ML_DH), F32)],
        compiler_params=_params("parallel", "parallel"),
        name="mlstm_mixer",
    )(pm, pm, pm, g_col, g_row, conv_w.astype(F32), conv_b.astype(F32).reshape(1, -1),
      w_q.astype(BF16), w_k.astype(BF16), w_v.astype(BF16),
      gb.reshape(nh, 1, 4), jnp.pad(gb, ((0, 0), (0, 4))).reshape(nh, 8, 1),
      skip.astype(F32).reshape(nh, 1, ML_DH), onorm.astype(F32).reshape(1, ML_DH))
    out = _outproj(y.reshape(bsz * seq, D_INNER), w_out, h2, final_g)
    return out.reshape(bsz, seq, D_MODEL)


def kernel(x, ssd_norm, ssd_w_in, ssd_conv_w, ssd_conv_b, ssd_dt_bias, ssd_a_log, ssd_d, ssd_gnorm, ssd_w_out, gla_norm, gla_w_in, gla_w_gate, gla_b_gate, gla_onorm, gla_w_out, hy_norm, hy_w_in, hy_conv_w, hy_conv_b, hy_ffn_w_in, hy_ffn_b_in, hy_ffn_w_hid, hy_ffn_b_hid, hy_ffn_freq, hy_ffn_w_out, hy_d, hy_w_out, ml_norm, ml_w_in, ml_conv_w, ml_conv_b, ml_w_q, ml_w_k, ml_w_v, ml_gate_b, ml_skip, ml_onorm, ml_w_out, final_norm):
    depth = ssd_norm.shape[0] + gla_norm.shape[0] + hy_norm.shape[0] + ml_norm.shape[0]
    h = x
    for i in range(depth):
        kind, j = i % 4, i // 4
        fg = final_norm if i == depth - 1 else None
        if kind == 0:
            h = _ssd_layer(h, ssd_norm[j], ssd_w_in[j], ssd_conv_w[j], ssd_conv_b[j], ssd_dt_bias[j],
                           ssd_a_log[j], ssd_d[j], ssd_gnorm[j], ssd_w_out[j], fg)
        elif kind == 1:
            h = _gla_layer(h, gla_norm[j], gla_w_in[j], gla_w_gate[j], gla_b_gate[j], gla_onorm[j],
                           gla_w_out[j], fg)
        elif kind == 2:
            h = _hyena_layer(h, hy_norm[j], hy_w_in[j], hy_conv_w[j], hy_conv_b[j], hy_ffn_w_in[j],
                             hy_ffn_b_in[j], hy_ffn_w_hid[j], hy_ffn_b_hid[j], hy_ffn_freq[j],
                             hy_ffn_w_out[j], hy_d[j], hy_w_out[j], fg)
        else:
            h = _mlstm_layer(h, ml_norm[j], ml_w_in[j], ml_conv_w[j], ml_conv_b[j], ml_w_q[j], ml_w_k[j],
                             ml_w_v[j], ml_gate_b[j], ml_skip[j], ml_onorm[j], ml_w_out[j], fg)
    return h
```

```python
import functools
import math

import jax
import jax.numpy as jnp
from jax import lax
from jax.experimental import pallas as pl
from jax.experimental.pallas import tpu as pltpu

F32 = jnp.float32
BF16 = jnp.bfloat16

D_MODEL = 1024
D_INNER = 2 * D_MODEL
EPS = 1e-6

SSD_HEAD_DIM = 64
SSD_HEADS = D_INNER // SSD_HEAD_DIM
SSD_GROUPS = 8
SSD_HPG = SSD_HEADS // SSD_GROUPS
SSD_STATE = 128
SSD_CHUNK = 128
SSD_GW = SSD_HPG * SSD_HEAD_DIM
SSD_BC = SSD_GROUPS * SSD_STATE

GLA_HEADS = 4
GLA_DK = D_MODEL // 2 // GLA_HEADS
GLA_DV = D_INNER // GLA_HEADS
GLA_RANK = 16
GLA_NORMALIZER = 16.0
GLA_CHUNK = 64
GLA_QK = GLA_HEADS * GLA_DK

HY_ORDER = 2
HY_EMB = 33
HY_BANDS = (HY_EMB - 1) // 2
HY_FFN = 64
HY_INNER = 2
HY_FAST_PCT = 0.3
HY_SLOW_PCT = 1.5
HY_TARGET = 1e-2

ML_HEADS = 4
ML_DH = D_INNER // ML_HEADS
ML_DK = ML_DH // 2
ML_CHUNK = 128

LANES = 128
INPROJ_TM = 512
INPROJ_TN = 1024
VMEM_LIMIT = 56 * 1024 * 1024

_NEG_INF = float("-inf")


def _params(*sem):
    return pltpu.CompilerParams(dimension_semantics=sem, vmem_limit_bytes=VMEM_LIMIT)


def _sigmoid(x):
    return 0.5 * jnp.tanh(0.5 * x) + 0.5


def _silu(x):
    half = 0.5 * x
    return half + half * jnp.tanh(half)


def _softplus(x):
    return jnp.maximum(x, 0.0) + jnp.log1p(jnp.exp(-jnp.abs(x)))


def _log_sigmoid(x):
    return -_softplus(-x)


def _dot(a, b):
    return jnp.dot(a, b, preferred_element_type=F32)


def _dot_nt(a, b):
    return lax.dot_general(a, b, (((1,), (1,)), ((), ())), preferred_element_type=F32)


def _dot_tn(a, b):
    return lax.dot_general(a, b, (((0,), (0,)), ((), ())), preferred_element_type=F32)


def _dot_f32(a, b):
    return jnp.dot(a, b, preferred_element_type=F32, precision=lax.Precision.HIGHEST)


def _split(v):
    hi = v.astype(BF16)
    lo = (v - hi.astype(F32)).astype(BF16)
    return hi, lo


def _tri_dot(t, v):
    hi, lo = _split(v)
    return _dot(t, hi) + _dot(t, lo)


def _dot_tri(v, t):
    hi, lo = _split(v)
    return _dot(hi, t) + _dot(lo, t)


def _tri_consts(q):
    ri = lax.broadcasted_iota(jnp.int32, (q, q), 0)
    ci = lax.broadcasted_iota(jnp.int32, (q, q), 1)
    lower = ri >= ci
    upper = ci >= ri
    t_lower = jnp.where(lower, 1.0, 0.0).astype(BF16)
    t_upper = jnp.where(upper, 1.0, 0.0).astype(BF16)
    return (lower, upper), (t_lower, t_upper)


def _shift_rows(x, k):
    if k == 0:
        return x
    n = x.shape[0]
    rows = lax.broadcasted_iota(jnp.int32, x.shape, 0)
    y = pltpu.roll(x, k % n, axis=0)
    if k > 0:
        return jnp.where(rows >= k, y, 0.0)
    return jnp.where(rows < n + k, y, 0.0)


def _dwconv(x, w, b):
    taps = w.shape[0]
    pad = (taps - 1) // 2
    acc = x * w[pad:pad + 1, :] + b
    for j in range(taps):
        if j != pad:
            acc = acc + _shift_rows(x, pad - j) * w[j:j + 1, :]
    return acc


def _local_unroll(nc):
    return next(u for u in (4, 2, 1) if nc % u == 0)


CONV_HALO = 8


def _conv_blocks(src_ref, pad_ref, w, b, rb, emit):
    seq = src_ref.shape[0]
    taps = w.shape[0]
    pad = (taps - 1) // 2
    assert pad < CONV_HALO
    zeros = jnp.zeros((CONV_HALO, pad_ref.shape[1]), F32)
    pad_ref[0:CONV_HALO, :] = zeros
    pad_ref[CONV_HALO + seq:CONV_HALO + seq + CONV_HALO, :] = zeros
    pad_ref[CONV_HALO:CONV_HALO + seq, :] = src_ref[...].astype(F32)
    n = rb + 2 * CONV_HALO
    for r0 in range(0, seq, rb):
        win = pad_ref[r0:r0 + n, :]
        acc = b
        for j in range(taps):
            shift = pad - j
            moved = win if shift == 0 else pltpu.roll(win, shift % n, axis=0)
            acc = acc + moved[CONV_HALO:CONV_HALO + rb, :] * w[j:j + 1, :]
        emit(r0, acc)


def _rms(x):
    return x * lax.rsqrt(jnp.mean(x * x, axis=-1, keepdims=True) + EPS)


def _inproj_kernel(*refs, tn, with_gates):
    if with_gates:
        x_ref, g_ref, w_ref, wg_ref, o_ref, og_ref = refs
    else:
        x_ref, g_ref, w_ref, o_ref = refs
    xn = (_rms(x_ref[...]) * g_ref[...]).astype(BF16)
    for c0 in range(0, w_ref.shape[1], tn):
        o_ref[:, c0:c0 + tn] = _dot(xn, w_ref[:, c0:c0 + tn]).astype(o_ref.dtype)
    if with_gates:
        og_ref[...] = _dot(xn, wg_ref[...])


def _inproj(x2d, g, w, w_gate, tn):
    m, k = x2d.shape
    n = w.shape[1]
    tm = min(INPROJ_TM, m)
    with_gates = w_gate is not None
    in_specs = [pl.BlockSpec((tm, k), lambda i: (i, 0)),
                pl.BlockSpec((1, k), lambda i: (0, 0)),
                pl.BlockSpec((k, n), lambda i: (0, 0), pipeline_mode=pl.Buffered(1))]
    out_specs = [pl.BlockSpec((tm, n), lambda i: (i, 0))]
    out_shape = [jax.ShapeDtypeStruct((m, n), BF16)]
    args = [x2d, g.reshape(1, k).astype(F32), w]
    if with_gates:
        in_specs.append(pl.BlockSpec((k, LANES), lambda i: (0, 0), pipeline_mode=pl.Buffered(1)))
        out_specs.append(pl.BlockSpec((tm, LANES), lambda i: (i, 0)))
        out_shape.append(jax.ShapeDtypeStruct((m, LANES), F32))
        args.append(w_gate)
    outs = pl.pallas_call(
        functools.partial(_inproj_kernel, tn=tn, with_gates=with_gates),
        grid=(m // tm,),
        in_specs=in_specs,
        out_specs=out_specs,
        out_shape=out_shape,
        compiler_params=_params("parallel"),
        name="inproj",
    )(*args)
    return (outs[0], outs[1]) if with_gates else (outs[0], None)


def _outproj_kernel(y_ref, w_ref, r_ref, g_ref, o_ref, *, final):
    acc = _dot(y_ref[...], w_ref[...]) + r_ref[...]
    if final:
        acc = _rms(acc) * g_ref[...]
    o_ref[...] = acc


def _outproj(y2d, w, res2d, final_g=None):
    m, k = y2d.shape
    n = w.shape[1]
    tm = min(512, m)
    final = final_g is not None
    g = (final_g if final else jnp.ones((n,), F32)).reshape(1, n).astype(F32)
    return pl.pallas_call(
        functools.partial(_outproj_kernel, final=final),
        grid=(m // tm,),
        in_specs=[pl.BlockSpec((tm, k), lambda i: (i, 0)),
                  pl.BlockSpec((k, n), lambda i: (0, 0)),
                  pl.BlockSpec((tm, n), lambda i: (i, 0)),
                  pl.BlockSpec((1, n), lambda i: (0, 0))],
        out_specs=pl.BlockSpec((tm, n), lambda i: (i, 0)),
        out_shape=jax.ShapeDtypeStruct((m, n), F32),
        compiler_params=_params("parallel"),
        name="outproj",
    )(y2d, w.astype(BF16), res2d, g)


def _pad_cols(w, n):
    return jnp.pad(w, ((0, 0), (0, n - w.shape[1])))


def _ssd_kernel(z_ref, x_ref, b_ref, c_ref, dtc_ref, dtr_ref,
                cwx_ref, cwb_ref, cwc_ref, cbx_ref, cbb_ref, cbc_ref,
                biasr_ref, biasc_ref, alogr_ref, alogc_ref, dskip_ref, gnorm_ref,
                o_ref, xs_ref, cs_ref, bt_ref, y_ref, s_ref, u_ref, e1_ref, dec_ref, padx_ref, padn_ref):
    q = SSD_CHUNK
    seq = x_ref.shape[0]
    nc = seq // q
    hpg = SSD_HPG
    assert hpg * SSD_HEAD_DIM == 2 * LANES
    (lower, upper), (t_lower, t_upper) = _tri_consts(q)
    lane_head = lax.shift_right_logical(
        lax.broadcasted_iota(jnp.int32, (1, SSD_GW), 1), int(math.log2(SSD_HEAD_DIM)))
    first_head = lax.broadcasted_iota(jnp.int32, (1, LANES), 1) < SSD_HEAD_DIM
    fwd_lane = lax.broadcasted_iota(jnp.int32, (1, 2 * hpg), 1) < hpg
    fwd_row = lax.broadcasted_iota(jnp.int32, (2 * hpg, 1), 0) < hpg

    def emit_x(r0, blk):
        xs_ref[r0:r0 + q, :] = _silu(blk)

    def emit_c(r0, blk):
        cs_ref[r0:r0 + q, :] = _silu(blk).astype(BF16)

    def emit_b(r0, blk):
        bt_ref[r0 // q] = _silu(blk).T.astype(BF16)

    _conv_blocks(x_ref, padx_ref, cwx_ref[...], cbx_ref[...], q, emit_x)
    _conv_blocks(c_ref, padn_ref, cwc_ref[...], cbc_ref[...], q, emit_c)
    _conv_blocks(b_ref, padn_ref, cwb_ref[...], cbb_ref[...], q, emit_b)

    bias_row = biasr_ref[...]
    bias_col = biasc_ref[...]
    a_row = -jnp.exp(alogr_ref[...])
    a_col = -jnp.exp(alogc_ref[...])
    dskip = dskip_ref[...]

    def rep(cols, j):
        return jnp.broadcast_to(cols[:, j:j + 1], (cols.shape[0], LANES))

    sel_dt = jnp.where(
        lax.broadcasted_iota(jnp.int32, (2 * hpg, 2 * SSD_GW), 0)
        == lax.shift_right_logical(lax.broadcasted_iota(jnp.int32, (2 * hpg, 2 * SSD_GW), 1),
                                   int(math.log2(SSD_HEAD_DIM))), 1.0, 0.0).astype(BF16)

    unroll = _local_unroll(nc)

    def local_body(i, carry):
        chunks = [i * unroll + j for j in range(unroll)]
        rows = [pl.ds(pl.multiple_of(c * q, q), q) for c in chunks]
        us = range(unroll)
        dt_col = [_softplus(dtc_ref[r, :] + bias_row) for r in rows]
        dt_row = [_softplus(dtr_ref[c] + bias_col) for c in chunks]
        la_col = [dt * a_row for dt in dt_col]
        la_row = [dt * a_col for dt in dt_row]
        pre_col = [_tri_dot(t_lower, la) for la in la_col]
        pre_row = [_dot_tri(la, t_upper) for la in la_row]
        cbs = [_dot(cs_ref[rows[j], :], bt_ref[chunks[j]]) for j in us]
        dt_exp = [_dot(dt.astype(BF16), sel_dt) for dt in dt_col]
        ys, xqs, cum_exps = [], [], []
        for j in us:
            cum_col = jnp.where(fwd_lane, pre_col[j], pre_col[j][q - 1:q, :] - pre_col[j] + la_col[j])
            cum_row = jnp.where(fwd_row, pre_row[j], pre_row[j][:, q - 1:q] - pre_row[j] + la_row[j])
            reps = [rep(cum_col, r) for r in range(2 * hpg)]
            ms = []
            for r in range(hpg):
                rb = hpg + r
                mf = jnp.exp(jnp.where(lower, reps[r] - cum_row[r:r + 1, :], _NEG_INF)) * dt_row[j][r:r + 1, :]
                mb = jnp.exp(jnp.where(upper, reps[rb] - cum_row[rb:rb + 1, :], _NEG_INF)) * dt_row[j][rb:rb + 1, :]
                ms.append((cbs[j] * (mf + mb)).astype(BF16))
            xq = xs_ref[rows[j], :]
            xb = xq.astype(BF16)
            xbd = jnp.concatenate(
                [jnp.where(lane_head == r, xb, jnp.zeros_like(xb)) for r in range(hpg)], axis=0)
            ys.append(_dot(jnp.concatenate(ms, axis=1), xbd))
            xqs.append(xq)
            cum_exps.append([jnp.concatenate(
                [jnp.where(first_head, reps[lo], reps[lo + 1]),
                 jnp.where(first_head, reps[lo + 2], reps[lo + 3])], axis=1) for lo in (0, hpg)])
        for j in us:
            y_ref[rows[j], :] = ys[j] + xqs[j] * dskip
            for d in range(2):
                cum_exp = cum_exps[j][d]
                tot = cum_exp[q - 1:q, :] if d == 0 else cum_exp[0:1, :]
                w = dt_exp[j][:, d * SSD_GW:(d + 1) * SSD_GW] * jnp.exp(tot - cum_exp)
                u_ref[d, chunks[j]] = _dot(bt_ref[chunks[j]], (xqs[j] * w).astype(BF16))
                e1_ref[d, rows[j], :] = jnp.exp(cum_exp).astype(BF16)
                dec_ref[d, chunks[j]] = jnp.exp(tot)
        return carry

    lax.fori_loop(0, nc // unroll, local_body, 0)

    s_ref[...] = jnp.zeros_like(s_ref)

    def serial(i, carry):
        chunks = (i, nc - 1 - i)
        rows = [pl.ds(pl.multiple_of(c * q, q), q) for c in chunks]
        states = [s_ref[d] for d in range(2)]
        inter = [_dot(cs_ref[rows[d], :], states[d].astype(BF16)) for d in range(2)]
        for d in range(2):
            s_ref[d] = states[d] * dec_ref[d, chunks[d]] + u_ref[d, chunks[d]]
        for d in range(2):
            y_ref[rows[d], :] += inter[d] * e1_ref[d, rows[d], :].astype(F32)
        return carry

    lax.fori_loop(0, nc, serial, 0)

    y = y_ref[...] * _silu(z_ref[...].astype(F32))
    o_ref[...] = (_rms(y) * gnorm_ref[...]).astype(o_ref.dtype)


def _ssd_layer(h, norm, w_in, conv_w, conv_b, dt_bias, a_log, d_skip, gnorm, w_out, final_g):
    bsz, seq, _ = h.shape
    g, hpg, q = SSD_GROUPS, SSD_HPG, SSD_CHUNK
    nc = seq // q
    h2 = h.reshape(bsz * seq, D_MODEL)
    n_main = 2 * D_INNER + 2 * SSD_BC
    pm, gates = _inproj(h2, norm, w_in[:, :n_main].astype(BF16),
                        _pad_cols(w_in[:, n_main:], LANES).astype(BF16), INPROJ_TN)
    pm = pm.reshape(bsz, seq, n_main)
    dt = gates[:, :2 * SSD_HEADS].reshape(bsz, seq, 2, g, hpg)
    dt_col = dt.transpose(0, 3, 1, 2, 4).reshape(bsz, g, seq, 2 * hpg)
    dt_row = dt.reshape(bsz, nc, q, 2, g, hpg).transpose(0, 4, 1, 3, 5, 2).reshape(bsz, g, nc, 2 * hpg, q)

    def per_group(p):
        return p.astype(F32).reshape(2, g, hpg).transpose(1, 0, 2).reshape(g, 2 * hpg)

    bias, alog = per_group(dt_bias), per_group(a_log)
    dskip = jnp.repeat(d_skip.astype(F32).reshape(g, hpg), SSD_HEAD_DIM, axis=1).reshape(g, 1, SSD_GW)
    cw = conv_w.astype(F32)
    cb = conv_b.astype(F32).reshape(1, -1)
    taps = cw.shape[0]
    xo = D_INNER // SSD_GW
    bo = 2 * D_INNER // SSD_STATE
    co = bo + SSD_GROUPS
    cbo = D_INNER // SSD_STATE
    cco = cbo + SSD_GROUPS

    y = pl.pallas_call(
        _ssd_kernel,
        grid=(bsz, g),
        in_specs=[
            pl.BlockSpec((None, seq, SSD_GW), lambda b, j: (b, 0, j)),
            pl.BlockSpec((None, seq, SSD_GW), lambda b, j: (b, 0, xo + j)),
            pl.BlockSpec((None, seq, SSD_STATE), lambda b, j: (b, 0, bo + j)),
            pl.BlockSpec((None, seq, SSD_STATE), lambda b, j: (b, 0, co + j)),
            pl.BlockSpec((None, None, seq, 2 * hpg), lambda b, j: (b, j, 0, 0)),
            pl.BlockSpec((None, None, nc, 2 * hpg, q), lambda b, j: (b, j, 0, 0, 0)),
            pl.BlockSpec((taps, SSD_GW), lambda b, j: (0, j)),
            pl.BlockSpec((taps, SSD_STATE), lambda b, j: (0, cbo + j)),
            pl.BlockSpec((taps, SSD_STATE), lambda b, j: (0, cco + j)),
            pl.BlockSpec((1, SSD_GW), lambda b, j: (0, j)),
            pl.BlockSpec((1, SSD_STATE), lambda b, j: (0, cbo + j)),
            pl.BlockSpec((1, SSD_STATE), lambda b, j: (0, cco + j)),
            pl.BlockSpec((None, 1, 2 * hpg), lambda b, j: (j, 0, 0)),
            pl.BlockSpec((None, 2 * hpg, 1), lambda b, j: (j, 0, 0)),
            pl.BlockSpec((None, 1, 2 * hpg), lambda b, j: (j, 0, 0)),
            pl.BlockSpec((None, 2 * hpg, 1), lambda b, j: (j, 0, 0)),
            pl.BlockSpec((None, 1, SSD_GW), lambda b, j: (j, 0, 0)),
            pl.BlockSpec((None, 1, SSD_GW), lambda b, j: (j, 0, 0)),
        ],
        out_specs=pl.BlockSpec((None, seq, SSD_GW), lambda b, j: (b, 0, j)),
        out_shape=jax.ShapeDtypeStruct((bsz, seq, D_INNER), BF16),
        scratch_shapes=[pltpu.VMEM((seq, SSD_GW), F32),
                        pltpu.VMEM((seq, SSD_STATE), BF16),
                        pltpu.VMEM((nc, SSD_STATE, q), BF16),
                        pltpu.VMEM((seq, SSD_GW), F32),
                        pltpu.VMEM((2, SSD_STATE, SSD_GW), F32),
                        pltpu.VMEM((2, nc, SSD_STATE, SSD_GW), F32),
                        pltpu.VMEM((2, seq, SSD_GW), BF16),
                        pltpu.VMEM((2, nc, 1, SSD_GW), F32),
                        pltpu.VMEM((seq + 2 * CONV_HALO, SSD_GW), F32),
                        pltpu.VMEM((seq + 2 * CONV_HALO, SSD_STATE), F32)],
        compiler_params=_params("parallel", "parallel"),
        name="ssd_mixer",
    )(pm, pm, pm, pm, dt_col, dt_row, cw, cw, cw, cb, cb, cb,
      bias.reshape(g, 1, 2 * hpg), bias.reshape(g, 2 * hpg, 1),
      alog.reshape(g, 1, 2 * hpg), alog.reshape(g, 2 * hpg, 1),
      dskip, gnorm.astype(F32).reshape(g, 1, SSD_GW))
    out = _outproj(y.reshape(bsz * seq, D_INNER), w_out, h2, final_g)
    return out.reshape(bsz, seq, D_MODEL)


def _gla_kernel(q_ref, k_ref, v_ref, z_ref, gl_ref, wg_ref, bg_ref, onorm_ref,
                o_ref, lg_ref, acc_ref, s_ref, qg_ref, u_ref, dec_ref):
    q = GLA_CHUNK
    seq = q_ref.shape[0]
    nc = seq // q
    masks, tris = _tri_consts(q)
    gl = gl_ref[...]
    for d in range(2):
        lg_ref[d] = _log_sigmoid(_dot_tri(gl, wg_ref[d]) + bg_ref[d]) * (1.0 / GLA_NORMALIZER)

    unroll = _local_unroll(nc)

    def local_body(i, carry):
        chunks = [i * unroll + j for j in range(unroll)]
        rows = [pl.ds(pl.multiple_of(c * q, q), q) for c in chunks]
        pairs = [(j, d) for j in range(unroll) for d in range(2)]
        cums = {(j, d): _tri_dot(tris[d], lg_ref[d, rows[j], :]) for j, d in pairs}
        qgs, kgs, kds = {}, {}, {}
        for j, d in pairs:
            cum = cums[j, d]
            qc = q_ref[rows[j], :].astype(F32) * (GLA_DK ** -0.5)
            kc = k_ref[rows[j], :].astype(F32)
            tot = cum[q - 1:q, :] if d == 0 else cum[0:1, :]
            qgs[j, d] = (qc * jnp.exp(cum)).astype(BF16)
            kgs[j, d] = (kc * jnp.exp(-cum)).astype(BF16)
            kds[j, d] = (kc * jnp.exp(tot - cum)).astype(BF16)
            qg_ref[d, rows[j], :] = qgs[j, d]
            dec_ref[d, chunks[j]] = jnp.exp(tot)
        atts = {p: _dot_nt(qgs[p], kgs[p]) for p in pairs}
        for j, d in pairs:
            u_ref[d, chunks[j]] = _dot_tn(v_ref[rows[j], :], kds[j, d]).astype(BF16)
        atts = {(j, d): jnp.where(masks[d], atts[j, d], 0.0).astype(BF16) for j, d in pairs}
        outs = {(j, d): _dot(atts[j, d], v_ref[rows[j], :]) for j, d in pairs}
        for j in range(unroll):
            acc_ref[rows[j], :] = outs[j, 0] + outs[j, 1]
        return carry

    lax.fori_loop(0, nc // unroll, local_body, 0)

    s_ref[...] = jnp.zeros_like(s_ref)

    def serial(i, carry):
        chunks = (i, nc - 1 - i)
        rows = [pl.ds(pl.multiple_of(c * q, q), q) for c in chunks]
        states = [s_ref[d] for d in range(2)]
        inter = [_dot_nt(qg_ref[d, rows[d], :], states[d].astype(BF16)) for d in range(2)]
        for d in range(2):
            s_ref[d] = states[d] * dec_ref[d, chunks[d]] + u_ref[d, chunks[d]].astype(F32)
        for d in range(2):
            acc_ref[rows[d], :] += inter[d]
        return carry

    lax.fori_loop(0, nc, serial, 0)

    out = _rms(acc_ref[...]) * onorm_ref[...]
    o_ref[...] = (out * _silu(z_ref[...].astype(F32))).astype(o_ref.dtype)


def _gla_layer(h, norm, w_in, w_gate, b_gate, onorm, w_out, final_g):
    bsz, seq, _ = h.shape
    nh = GLA_HEADS
    h2 = h.reshape(bsz * seq, D_MODEL)
    n_main = 2 * GLA_QK + 2 * D_INNER
    pm, gates = _inproj(h2, norm, w_in[:, :n_main].astype(BF16),
                        _pad_cols(w_in[:, n_main:], LANES).astype(BF16), INPROJ_TN)
    pm = pm.reshape(bsz, seq, n_main)
    gates = gates.reshape(bsz, seq, LANES)
    wg = jnp.zeros((2, LANES, GLA_QK), F32)
    for d in range(2):
        wg = wg.at[d, d * GLA_RANK:(d + 1) * GLA_RANK, :].set(w_gate[d].astype(F32))
    vo = 2 * GLA_QK // GLA_DV
    zo = vo + nh
    y = pl.pallas_call(
        _gla_kernel,
        grid=(bsz, nh),
        in_specs=[
            pl.BlockSpec((None, seq, GLA_DK), lambda b, j: (b, 0, j)),
            pl.BlockSpec((None, seq, GLA_DK), lambda b, j: (b, 0, nh + j)),
            pl.BlockSpec((None, seq, GLA_DV), lambda b, j: (b, 0, vo + j)),
            pl.BlockSpec((None, seq, GLA_DV), lambda b, j: (b, 0, zo + j)),
            pl.BlockSpec((None, seq, LANES), lambda b, j: (b, 0, 0)),
            pl.BlockSpec((2, LANES, GLA_DK), lambda b, j: (0, 0, j)),
            pl.BlockSpec((2, 1, GLA_DK), lambda b, j: (0, 0, j)),
            pl.BlockSpec((1, GLA_DV), lambda b, j: (0, 0)),
        ],
        out_specs=pl.BlockSpec((None, seq, GLA_DV), lambda b, j: (b, 0, j)),
        out_shape=jax.ShapeDtypeStruct((bsz, seq, D_INNER), BF16),
        scratch_shapes=[pltpu.VMEM((2, seq, GLA_DK), F32),
                        pltpu.VMEM((seq, GLA_DV), F32),
                        pltpu.VMEM((2, GLA_DV, GLA_DK), F32),
                        pltpu.VMEM((2, seq, GLA_DK), BF16),
                        pltpu.VMEM((2, seq // GLA_CHUNK, GLA_DV, GLA_DK), BF16),
                        pltpu.VMEM((2, seq // GLA_CHUNK, 1, GLA_DK), F32)],
        compiler_params=_params("parallel", "parallel"),
        name="gla_mixer",
    )(pm, pm, pm, pm, gates, wg.astype(BF16), b_gate.astype(F32).reshape(2, 1, GLA_QK),
      onorm.astype(F32).reshape(1, GLA_DV))
    out = _outproj(y.reshape(bsz * seq, D_INNER), w_out, h2, final_g)
    return out.reshape(bsz, seq, D_MODEL)


def _hy_mlp_kernel(f_ref, w0_ref, b0_ref, w1_ref, b1_ref, w2_ref, b2_ref, fr_ref, o_ref):
    fr = fr_ref[...]
    hid = jnp.sin(fr[0:1, :] * (_dot_f32(f_ref[...], w0_ref[...]) + b0_ref[...]))
    hid = jnp.sin(fr[1:2, :] * (_dot_f32(hid, w1_ref[...]) + b1_ref[...]))
    hid = jnp.sin(fr[2:3, :] * (_dot_f32(hid, w2_ref[...]) + b2_ref[...]))
    o_ref[...] = hid


def _hy_spec_kernel(hid_ref, wf_ref, wb_ref, t_ref, dl_ref, cos_ref, sin_ref, cosr_ref, sinr_ref,
                    kre_ref, ks_ref, p_ref):
    lb = hid_ref.shape[0] // 2

    @pl.when(pl.program_id(2) == 0)
    def _():
        env = jnp.exp(-t_ref[...] * dl_ref[...])
        hf = _dot_f32(hid_ref[...], wf_ref[...]) * env
        hb = _dot_f32(hid_ref[...], wb_ref[...]) * env
        first = lax.broadcasted_iota(jnp.int32, (lb, hf.shape[1]), 0) == 0
        hf_lo0 = jnp.where(first, 0.0, hf[0:lb])
        hb_lo0 = jnp.where(first, 0.0, hb[0:lb])
        p_ref[0] = (hf[0:lb] + hb_lo0).astype(BF16)
        p_ref[1] = (hf[0:lb] - hb_lo0).astype(BF16)
        p_ref[2] = hf[lb:].astype(BF16)
        p_ref[3] = hf_lo0.astype(BF16)
        p_ref[4] = hb[lb:].astype(BF16)
        p_ref[5] = hb_lo0.astype(BF16)

    scale = 1.0 / lb
    cos, sin, cosr, sinr = cos_ref[...], sin_ref[...], cosr_ref[...], sinr_ref[...]
    dt = kre_ref.dtype
    kre_ref[0] = (_dot(cos, p_ref[0]) * scale).astype(dt)
    ks_ref[0] = (_dot(sin, p_ref[1]) * scale).astype(dt)
    kre_ref[1] = ((_dot(cos, p_ref[2]) + _dot(cosr, p_ref[3])) * scale).astype(dt)
    ks_ref[1] = ((_dot(sin, p_ref[2]) + _dot(sinr, p_ref[3])) * scale).astype(dt)
    kre_ref[2] = ((_dot(cos, p_ref[4]) + _dot(cosr, p_ref[5])) * scale).astype(dt)
    ks_ref[2] = ((_dot(sin, p_ref[4]) + _dot(sinr, p_ref[5])) * (-scale)).astype(dt)


def _hy_conv_kernel(v_ref, x1_ref, x2_ref, z_ref, cwv_ref, cw1_ref, cw2_ref,
                    cbv_ref, cb1_ref, cb2_ref, kre_ref, ks_ref, d_ref, w_ref, o_ref,
                    y_ref, yb_ref, zf_ref, g_ref, yf_ref, *, fb):
    seq, tc = v_ref.shape
    lb = seq // 2
    nblk = lb // fb
    sb = min(64, fb)
    y_ref[...] = _dwconv(v_ref[...].astype(F32), cwv_ref[...], cbv_ref[...])
    gates = ((x1_ref, cw1_ref, cb1_ref), (x2_ref, cw2_ref, cb2_ref))
    for o, (x_ref, cw_ref, cb_ref) in enumerate(gates):
        yb_ref[:, 0:tc] = y_ref[0:lb, :].astype(BF16)
        yb_ref[:, tc:2 * tc] = y_ref[lb:seq, :].astype(BF16)
        g_ref[...] = _dwconv(x_ref[...].astype(F32), cw_ref[...], cb_ref[...])

        def forward(k, carry, o=o):
            f0 = pl.multiple_of(k * fb, fb)
            yb = yb_ref[...]
            yf_ref[0] = _dot(w_ref[pl.ds(f0, fb), :], yb)
            yf_ref[1] = _dot(w_ref[pl.ds(lb + f0, fb), :], yb)
            for s0 in range(0, fb, sb):
                re = [yf_ref[0, s0:s0 + sb, 0:tc], yf_ref[0, s0:s0 + sb, tc:2 * tc]]
                im = [yf_ref[1, s0:s0 + sb, 0:tc], yf_ref[1, s0:s0 + sb, tc:2 * tc]]
                frows = pl.ds(pl.multiple_of(f0 + s0, sb), sb)
                kre = [kre_ref[o, n, frows, :].astype(F32) for n in range(3)]
                ks = [ks_ref[o, n, frows, :].astype(F32) for n in range(3)]
                for i, (n0, n1) in enumerate(((0, 2), (1, 0))):
                    zre = re[0] * kre[n0] - im[0] * ks[n0] + re[1] * kre[n1] - im[1] * ks[n1]
                    zim = im[0] * kre[n0] + re[0] * ks[n0] + im[1] * kre[n1] + re[1] * ks[n1]
                    zf_ref[frows, i * tc:(i + 1) * tc] = zre.astype(BF16)
                    zf_ref[pl.ds(pl.multiple_of(lb + f0 + s0, sb), sb), i * tc:(i + 1) * tc] = zim.astype(BF16)
            return carry

        lax.fori_loop(0, nblk, forward, 0)

        def inverse(k, carry, o=o):
            t0 = pl.multiple_of(k * fb, fb)
            conv = (_dot(w_ref[pl.ds(t0, fb), :], zf_ref[0:lb, :])
                    + _dot(w_ref[pl.ds(lb + t0, fb), :], zf_ref[lb:2 * lb, :]))
            for i in range(2):
                rows = pl.ds(i * lb + t0, fb)
                y_ref[rows, :] = g_ref[rows, :] * (conv[:, i * tc:(i + 1) * tc]
                                                   + y_ref[rows, :] * d_ref[o:o + 1, :])
            return carry

        lax.fori_loop(0, nblk, inverse, 0)
    o_ref[...] = (y_ref[...] * _silu(z_ref[...].astype(F32))).astype(o_ref.dtype)


def _hyena_layer(h, norm, w_in, conv_w, conv_b, ffn_w_in, ffn_b_in, ffn_w_hid, ffn_b_hid,
                 ffn_freq, ffn_w_out, d_bias, w_out, final_g):
    bsz, seq, _ = h.shape
    c = D_INNER
    h2 = h.reshape(bsz * seq, D_MODEL)
    pm = _inproj(h2, norm, w_in.astype(BF16), None, INPROJ_TN)[0].reshape(bsz, seq, 4 * c)

    t = jnp.linspace(0.0, 1.0, seq, dtype=F32)[:, None]
    pos = jnp.arange(seq, dtype=F32)[:, None]
    bands = jnp.linspace(1e-4, HY_BANDS - 1, HY_BANDS, dtype=F32)[None]
    ang = (2.0 * math.pi / seq) * pos * bands
    feats = jnp.concatenate([t, jnp.cos(ang), -jnp.sin(ang)], axis=-1)
    feats = _pad_cols(feats, LANES)
    w0 = jnp.pad(ffn_w_in.astype(F32), ((0, LANES - HY_EMB), (0, 0)))
    hid = pl.pallas_call(
        _hy_mlp_kernel,
        out_shape=jax.ShapeDtypeStruct((seq, HY_FFN), F32),
        name="hyena_filter_mlp",
    )(feats, w0, ffn_b_in.astype(F32).reshape(1, HY_FFN),
      ffn_w_hid[0].astype(F32), ffn_b_hid[0].astype(F32).reshape(1, HY_FFN),
      ffn_w_hid[1].astype(F32), ffn_b_hid[1].astype(F32).reshape(1, HY_FFN),
      ffn_freq.astype(F32))

    max_decay = math.log(HY_TARGET) / HY_FAST_PCT
    min_decay = math.log(HY_TARGET) / HY_SLOW_PCT
    deltas = jnp.abs(jnp.linspace(min_decay, max_decay, c, dtype=F32)).reshape(1, c)

    lb = seq // 2
    fi = jnp.arange(lb, dtype=jnp.int32)
    n4 = 8 * lb

    def table(num, fn):
        return fn((2.0 * math.pi / n4) * (num % n4).astype(F32)).astype(BF16)

    num1 = (2 * fi[:, None] + 1) * (2 * fi[None, :])
    numr = (2 * fi[:, None] + 1) * (2 * (fi[None, :] - lb))
    num2 = (2 * fi[:, None] + 1) * (2 * fi[None, :] + 1)
    cos1, sin1 = table(num1, jnp.cos), table(num1, jnp.sin)
    cosr, sinr = table(numr, jnp.cos), table(numr, jnp.sin)
    w2 = jnp.concatenate([table(num2, jnp.cos), table(num2, jnp.sin)], axis=0)

    tc = min(512, c)
    fb = min(512, lb)
    nct = c // tc
    wo = ffn_w_out.astype(F32)
    tab_spec = pl.BlockSpec((fb, lb), lambda o, j, k: (k, 0))
    kre, ks = pl.pallas_call(
        _hy_spec_kernel,
        grid=(HY_ORDER, nct, lb // fb),
        in_specs=[
            pl.BlockSpec((seq, HY_FFN), lambda o, j, k: (0, 0)),
            pl.BlockSpec((HY_FFN, tc), lambda o, j, k: (0, o * nct + j)),
            pl.BlockSpec((HY_FFN, tc), lambda o, j, k: (0, (HY_ORDER + o) * nct + j)),
            pl.BlockSpec((seq, 1), lambda o, j, k: (0, 0)),
            pl.BlockSpec((1, tc), lambda o, j, k: (0, j)),
            tab_spec, tab_spec, tab_spec, tab_spec,
        ],
        out_specs=[pl.BlockSpec((None, 3, fb, tc), lambda o, j, k: (o, 0, k, j)),
                   pl.BlockSpec((None, 3, fb, tc), lambda o, j, k: (o, 0, k, j))],
        out_shape=[jax.ShapeDtypeStruct((HY_ORDER, 3, lb, c), BF16)] * 2,
        scratch_shapes=[pltpu.VMEM((6, lb, tc), BF16)],
        compiler_params=_params("parallel", "parallel", "arbitrary"),
        name="hyena_filter_spectrum",
    )(hid, wo, wo, t, deltas, cos1, sin1, cosr, sinr)

    tcc = min(256, c)
    ncc = c // tcc
    cw = conv_w.astype(F32)
    cb = conv_b.astype(F32).reshape(1, -1)
    taps = cw.shape[0]
    y = pl.pallas_call(
        functools.partial(_hy_conv_kernel, fb=fb),
        grid=(ncc, bsz),
        in_specs=[
            pl.BlockSpec((None, seq, tcc), lambda j, b: (b, 0, j)),
            pl.BlockSpec((None, seq, tcc), lambda j, b: (b, 0, ncc + j)),
            pl.BlockSpec((None, seq, tcc), lambda j, b: (b, 0, 2 * ncc + j)),
            pl.BlockSpec((None, seq, tcc), lambda j, b: (b, 0, 3 * ncc + j)),
            pl.BlockSpec((taps, tcc), lambda j, b: (0, j)),
            pl.BlockSpec((taps, tcc), lambda j, b: (0, ncc + j)),
            pl.BlockSpec((taps, tcc), lambda j, b: (0, 2 * ncc + j)),
            pl.BlockSpec((1, tcc), lambda j, b: (0, j)),
            pl.BlockSpec((1, tcc), lambda j, b: (0, ncc + j)),
            pl.BlockSpec((1, tcc), lambda j, b: (0, 2 * ncc + j)),
            pl.BlockSpec((HY_ORDER, 3, lb, tcc), lambda j, b: (0, 0, 0, j), pipeline_mode=pl.Buffered(1)),
            pl.BlockSpec((HY_ORDER, 3, lb, tcc), lambda j, b: (0, 0, 0, j), pipeline_mode=pl.Buffered(1)),
            pl.BlockSpec((HY_ORDER, tcc), lambda j, b: (0, j)),
            pl.BlockSpec((2 * lb, lb), lambda j, b: (0, 0), pipeline_mode=pl.Buffered(1)),
        ],
        out_specs=pl.BlockSpec((None, seq, tcc), lambda j, b: (b, 0, j)),
        out_shape=jax.ShapeDtypeStruct((bsz, seq, c), BF16),
        scratch_shapes=[pltpu.VMEM((seq, tcc), F32),
                        pltpu.VMEM((lb, 2 * tcc), BF16),
                        pltpu.VMEM((2 * lb, 2 * tcc), BF16),
                        pltpu.VMEM((seq, tcc), F32),
                        pltpu.VMEM((2, fb, 2 * tcc), F32)],
        compiler_params=_params("parallel", "parallel"),
        name="hyena_mixer",
    )(pm, pm, pm, pm, cw, cw, cw, cb, cb, cb, kre, ks, d_bias.astype(F32), w2)
    out = _outproj(y.reshape(bsz * seq, c), w_out, h2, final_g)
    return out.reshape(bsz, seq, D_MODEL)


def _mlstm_kernel(xm_ref, z_ref, og_ref, gc_ref, gr_ref, cw_ref, cb_ref, wq_ref, wk_ref, wv_ref,
                  gbr_ref, gbc_ref, skip_ref, onorm_ref,
                  o_ref, q_ref, k_ref, v_ref, ch_ref, acc_ref, c_ref, nl_ref, strep_ref, kt_ref, str_ref, cmat_ref,
                  par_ref, sc_ref, rowc_ref, chunkc_ref):
    q = ML_CHUNK
    seq = xm_ref.shape[0]
    nc = seq // q
    masks, tris = _tri_consts(q)

    for j in range(ML_DH // LANES):
        sl = slice(j * LANES, (j + 1) * LANES)
        ch_ref[:, sl] = _silu(_dwconv(xm_ref[:, sl].astype(F32), cw_ref[:, sl], cb_ref[:, sl])).astype(BF16)
    ch = ch_ref[...]
    q_ref[...] = _dot(ch, wq_ref[...]).astype(BF16)
    kf = _dot(ch, wk_ref[...]) * (ML_DK ** -0.5)
    k_ref[...] = kf.astype(BF16)
    for c in range(nc):
        kt_ref[c] = kf[c * q:(c + 1) * q, :].T.astype(BF16)
    v_ref[...] =_dot(xm_ref[...], wv_ref[...]).astype(BF16)

    gb_row = gbr_ref[...]
    gb_col = gbc_ref[...]

    (t_lower, t_upper) = tris

    unroll = _local_unroll(nc)

    def local_body(i, carry):
        chunks = [i * unroll + j for j in range(unroll)]
        rows = [pl.ds(pl.multiple_of(c * q, q), q) for c in chunks]
        pairs = [(j, d) for j in range(unroll) for d in range(2)]
        g_col = [gc_ref[r, :] + gb_row for r in rows]
        g_row = [gr_ref[c] + gb_col for c in chunks]
        lf_col = [_log_sigmoid(g) for g in g_col]
        lf_row = [_log_sigmoid(g) for g in g_row]
        tri_c, tri_r = (t_lower, t_upper), (t_upper, t_lower)
        b_cols = {(j, d): _tri_dot(tri_c[d], lf_col[j]) for j, d in pairs}
        b_rows = {(j, d): _dot_tri(lf_row[j], tri_r[d]) for j, d in pairs}
        qk = [_dot_nt(q_ref[r, :], k_ref[r, :]) for r in rows]
        ss = {}
        for j, d in pairs:
            i_col = g_col[j][:, 2 * d:2 * d + 1]
            i_row = g_row[j][2 * d:2 * d + 1, :]
            b_col = b_cols[j, d][:, 2 * d + 1:2 * d + 2]
            b_row = b_rows[j, d][2 * d + 1:2 * d + 2, :]
            logd = jnp.where(masks[d], b_col - b_row + i_row, _NEG_INF)
            m_loc = jnp.max(logd, axis=1, keepdims=True)
            s = qk[j] * jnp.exp(logd - m_loc)
            ss[j, d] = s.astype(BF16)
            den_loc = jnp.sum(s, axis=1, keepdims=True)
            tot = b_col[q - 1:q, :] if d == 0 else b_col[0:1, :]
            lw_col = tot - b_col + i_col
            lw_max = jnp.max(lw_col, axis=0, keepdims=True)
            for n, stat in enumerate((m_loc, den_loc, b_col)):
                strep_ref[d, n, rows[j], :] = jnp.broadcast_to(stat, (q, LANES))
            sc_ref[d, chunks[j], 0:1, :] = jnp.broadcast_to(tot, (1, LANES))
            sc_ref[d, chunks[j], 1:2, :] = jnp.broadcast_to(lw_max, (1, LANES))
            str_ref[d, chunks[j]] = jnp.broadcast_to(tot - b_row + i_row - lw_max, (8, q))
        for j, d in pairs:
            nl_ref[d, rows[j], :] = _dot(ss[j, d], v_ref[rows[j], :]).astype(BF16)
        return carry

    lax.fori_loop(0, nc // unroll, local_body, 0)

    c_ref[...] = jnp.zeros_like(c_ref)
    cmat_ref[...] = jnp.zeros_like(cmat_ref)
    acc_ref[...] = jnp.zeros_like(acc_ref)

    nblk = 4
    cb, ab = ML_DK // nblk, q // nblk

    def tile(x, n):
        return jnp.concatenate([x] * n, axis=1)

    def serial(i, carry):
        chunks = (i, nc - 1 - i)
        rows = [pl.ds(pl.multiple_of(c * q, q), q) for c in chunks]
        w_rows = [jnp.exp(str_ref[d, chunks[d]]).astype(BF16) for d in range(2)]
        for d in range(2):
            par_ref[d, q:q + ML_DK, :] = _dot(kt_ref[chunks[d]] * w_rows[d][0:1, :], v_ref[rows[d], :])
        for d in range(2):
            par_ref[d, 0:q, :] = _dot(q_ref[rows[d], :], cmat_ref[d])
        new = []
        for d in range(2):
            n_row, m = carry[d]
            ksum = _dot(w_rows[d], k_ref[rows[d], :])[0:1, :]
            qn = _dot_nt(q_ref[rows[d], :],
                         jnp.broadcast_to(n_row, (LANES, ML_DK)).astype(BF16))
            m_loc = strep_ref[d, 0, rows[d], :]
            den_loc = strep_ref[d, 1, rows[d], :]
            inter = strep_ref[d, 2, rows[d], :] + m
            m_row = jnp.maximum(m_loc, inter)
            a_loc = jnp.exp(m_loc - m_row)
            a_int = jnp.exp(inter - m_row)
            den = den_loc * a_loc + qn * a_int
            scale = 1.0 / jnp.maximum(jnp.abs(den), jnp.exp(-m_row))
            rowc_ref[d, 0] = a_loc * scale
            rowc_ref[d, 1] = a_int * scale
            tot = sc_ref[d, chunks[d], 0:1, :]
            lw_max = sc_ref[d, chunks[d], 1:2, :]
            m_new = jnp.maximum(tot + m, lw_max)
            dec = jnp.exp(tot + m - m_new)
            gain = jnp.exp(lw_max - m_new)
            chunkc_ref[d, 0:1, :] = tile(dec, ML_DH // LANES)
            chunkc_ref[d, 1:2, :] = tile(gain, ML_DH // LANES)
            new.append((n_row * tile(dec, ML_DK // LANES) + ksum * tile(gain, ML_DK // LANES), m_new))

        def update(b, inner):
            for d in range(2):
                r = pl.ds(pl.multiple_of(b * cb, cb), cb)
                cnew = (c_ref[d, r, :] * chunkc_ref[d, 0:1, :]
                        + par_ref[d, pl.ds(pl.multiple_of(q + b * cb, cb), cb), :] * chunkc_ref[d, 1:2, :])
                c_ref[d, r, :] = cnew
                cmat_ref[d, r, :] = cnew.astype(BF16)
                rl = pl.ds(pl.multiple_of(b * ab, ab), ab)
                ra = pl.ds(pl.multiple_of(chunks[d] * q + b * ab, ab), ab)
                acc_ref[ra, :] += (nl_ref[d, ra, :].astype(F32) * tile(rowc_ref[d, 0, rl, :], ML_DH // LANES)
                                   + par_ref[d, rl, :] * tile(rowc_ref[d, 1, rl, :], ML_DH // LANES))
            return inner

        lax.fori_loop(0, nblk, update, 0)
        return tuple(new)

    init = (jnp.zeros((1, ML_DK), F32), jnp.full((1, LANES), _NEG_INF, F32))
    lax.fori_loop(0, nc, serial, (init, init))

    hh = _sigmoid(og_ref[...].astype(F32)) * acc_ref[...]
    hh = _rms(hh) * onorm_ref[...] + skip_ref[...] * ch_ref[...].astype(F32)
    o_ref[...] = (hh * _silu(z_ref[...].astype(F32))).astype(o_ref.dtype)


def _mlstm_layer(h, norm, w_in, conv_w, conv_b, w_q, w_k, w_v, gate_b, skip, onorm, w_out, final_g):
    bsz, seq, _ = h.shape
    nh, q = ML_HEADS, ML_CHUNK
    nc = seq // q
    h2 = h.reshape(bsz * seq, D_MODEL)
    n_main = 3 * D_INNER
    pm, gates = _inproj(h2, norm, w_in[:, :n_main].astype(BF16),
                        _pad_cols(w_in[:, n_main:], LANES).astype(BF16), INPROJ_TN)
    pm = pm.reshape(bsz, seq, n_main)
    gt = gates[:, :4 * nh].reshape(bsz, seq, 2, 2, nh)
    g_col = gt.transpose(0, 4, 1, 2, 3).reshape(bsz, nh, seq, 4)
    g_row = gt.reshape(bsz, nc, q, 2, 2, nh).transpose(0, 5, 1, 3, 4, 2).reshape(bsz, nh, nc, 4, q)
    g_row = jnp.pad(g_row, ((0, 0), (0, 0), (0, 0), (0, 4), (0, 0)))
    gb =gate_b.astype(F32).transpose(2, 0, 1).reshape(nh, 4)
    taps = conv_w.shape[0]
    y = pl.pallas_call(
        _mlstm_kernel,
        grid=(bsz, nh),
        in_specs=[
            pl.BlockSpec((None, seq, ML_DH), lambda b, j: (b, 0, j)),
            pl.BlockSpec((None, seq, ML_DH), lambda b, j: (b, 0, nh + j)),
            pl.BlockSpec((None, seq, ML_DH), lambda b, j: (b, 0, 2 * nh + j)),
            pl.BlockSpec((None, None, seq, 4), lambda b, j: (b, j, 0, 0)),
            pl.BlockSpec((None, None, nc, 8, q), lambda b, j: (b, j, 0, 0, 0)),
            pl.BlockSpec((taps, ML_DH), lambda b, j: (0, j)),
            pl.BlockSpec((1, ML_DH), lambda b, j: (0, j)),
            pl.BlockSpec((None, ML_DH, ML_DK), lambda b, j: (j, 0, 0)),
            pl.BlockSpec((None, ML_DH, ML_DK), lambda b, j: (j, 0, 0)),
            pl.BlockSpec((None, ML_DH, ML_DH), lambda b, j: (j, 0, 0)),
            pl.BlockSpec((None, 1, 4), lambda b, j: (j, 0, 0)),
            pl.BlockSpec((None, 8, 1), lambda b, j: (j, 0, 0)),
            pl.BlockSpec((None, 1, ML_DH), lambda b, j: (j, 0, 0)),
            pl.BlockSpec((1, ML_DH), lambda b, j: (0, 0)),
        ],
        out_specs=pl.BlockSpec((None, seq, ML_DH), lambda b, j: (b, 0, j)),
        out_shape=jax.ShapeDtypeStruct((bsz, seq, D_INNER), BF16),
        scratch_shapes=[pltpu.VMEM((seq, ML_DK), BF16),
                        pltpu.VMEM((seq, ML_DK), BF16),
                        pltpu.VMEM((seq, ML_DH), BF16),
                        pltpu.VMEM((seq, ML_DH), BF16),
                        pltpu.VMEM((seq, ML_DH), F32),
                        pltpu.VMEM((2, ML_DK, ML_DH), F32),
                        pltpu.VMEM((2, seq, ML_DH), BF16),
                        pltpu.VMEM((2, 3, seq, LANES), F32),
                        pltpu.VMEM((nc, ML_DK, q), BF16),
                        pltpu.VMEM((2, nc, 8, q), F32),
                        pltpu.VMEM((2, ML_DK, ML_DH), BF16),
                        pltpu.VMEM((2, q + ML_DK, ML_DH), F32),
                        pltpu.VMEM((2, nc, 8, LANES), F32),
                        pltpu.VMEM((2, 2, q, LANES), F32),
                        pltpu.VMEM((2, 8, ML_DH), F32)],
        compiler_params=_params("parallel", "parallel"),
        name="mlstm_mixer",
    )(pm, pm, pm, g_col, g_row, conv_w.astype(F32), conv_b.astype(F32).reshape(1, -1),
      w_q.astype(BF16), w_k.astype(BF16), w_v.astype(BF16),
      gb.reshape(nh, 1, 4), jnp.pad(gb, ((0, 0), (0, 4))).reshape(nh, 8, 1),
      skip.astype(F32).reshape(nh, 1, ML_DH), onorm.astype(F32).reshape(1, ML_DH))
    out = _outproj(y.reshape(bsz * seq, D_INNER), w_out, h2, final_g)
    return out.reshape(bsz, seq, D_MODEL)


def kernel(x, ssd_norm, ssd_w_in, ssd_conv_w, ssd_conv_b, ssd_dt_bias, ssd_a_log, ssd_d, ssd_gnorm, ssd_w_out, gla_norm, gla_w_in, gla_w_gate, gla_b_gate, gla_onorm, gla_w_out, hy_norm, hy_w_in, hy_conv_w, hy_conv_b, hy_ffn_w_in, hy_ffn_b_in, hy_ffn_w_hid, hy_ffn_b_hid, hy_ffn_freq, hy_ffn_w_out, hy_d, hy_w_out, ml_norm, ml_w_in, ml_conv_w, ml_conv_b, ml_w_q, ml_w_k, ml_w_v, ml_gate_b, ml_skip, ml_onorm, ml_w_out, final_norm):
    depth = ssd_norm.shape[0] + gla_norm.shape[0] + hy_norm.shape[0] + ml_norm.shape[0]
    h = x
    for i in range(depth):
        kind, j = i % 4, i // 4
        fg = final_norm if i == depth - 1 else None
        if kind == 0:
            h = _ssd_layer(h, ssd_norm[j], ssd_w_in[j], ssd_conv_w[j], ssd_conv_b[j], ssd_dt_bias[j],
                           ssd_a_log[j], ssd_d[j], ssd_gnorm[j], ssd_w_out[j], fg)
        elif kind == 1:
            h = _gla_layer(h, gla_norm[j], gla_w_in[j], gla_w_gate[j], gla_b_gate[j], gla_onorm[j],
                           gla_w_out[j], fg)
        elif kind == 2:
            h = _hyena_layer(h, hy_norm[j], hy_w_in[j], hy_conv_w[j], hy_conv_b[j], hy_ffn_w_in[j],
                             hy_ffn_b_in[j], hy_ffn_w_hid[j], hy_ffn_b_hid[j], hy_ffn_freq[j],
                             hy_ffn_w_out[j], hy_d[j], hy_w_out[j], fg)
        else:
            h = _mlstm_layer(h, ml_norm[j], ml_w_in[j], ml_conv_w[j], ml_conv_b[j], ml_w_q[j], ml_w_k[j],
                             ml_w_v[j], ml_gate_b[j], ml_skip[j], ml_onorm[j], ml_w_out[j], fg)
    return h
```

```python
import functools
import math

import jax
import jax.numpy as jnp
from jax import lax
from jax.experimental import pallas as pl
from jax.experimental.pallas import tpu as pltpu

F32 = jnp.float32
BF16 = jnp.bfloat16

D_MODEL = 1024
D_INNER = 2 * D_MODEL
EPS = 1e-6

SSD_HEAD_DIM = 64
SSD_HEADS = D_INNER // SSD_HEAD_DIM
SSD_GROUPS = 8
SSD_HPG = SSD_HEADS // SSD_GROUPS
SSD_STATE = 128
SSD_CHUNK = 128
SSD_GW = SSD_HPG * SSD_HEAD_DIM
SSD_BC = SSD_GROUPS * SSD_STATE

GLA_HEADS = 4
GLA_DK = D_MODEL // 2 // GLA_HEADS
GLA_DV = D_INNER // GLA_HEADS
GLA_RANK = 16
GLA_NORMALIZER = 16.0
GLA_CHUNK = 64
GLA_QK = GLA_HEADS * GLA_DK

HY_ORDER = 2
HY_EMB = 33
HY_BANDS = (HY_EMB - 1) // 2
HY_FFN = 64
HY_INNER = 2
HY_FAST_PCT = 0.3
HY_SLOW_PCT = 1.5
HY_TARGET = 1e-2

ML_HEADS = 4
ML_DH = D_INNER // ML_HEADS
ML_DK = ML_DH // 2
ML_CHUNK = 128

LANES = 128
INPROJ_TM = 512
INPROJ_TN = 1024
VMEM_LIMIT = 56 * 1024 * 1024

_NEG_INF = float("-inf")


def _params(*sem):
    return pltpu.CompilerParams(dimension_semantics=sem, vmem_limit_bytes=VMEM_LIMIT)


def _sigmoid(x):
    return 0.5 * jnp.tanh(0.5 * x) + 0.5


def _silu(x):
    half = 0.5 * x
    return half + half * jnp.tanh(half)


def _softplus(x):
    return jnp.maximum(x, 0.0) + jnp.log1p(jnp.exp(-jnp.abs(x)))


def _log_sigmoid(x):
    return -_softplus(-x)


def _dot(a, b):
    return jnp.dot(a, b, preferred_element_type=F32)


def _dot_nt(a, b):
    return lax.dot_general(a, b, (((1,), (1,)), ((), ())), preferred_element_type=F32)


def _dot_tn(a, b):
    return lax.dot_general(a, b, (((0,), (0,)), ((), ())), preferred_element_type=F32)


def _dot_f32(a, b):
    return jnp.dot(a, b, preferred_element_type=F32, precision=lax.Precision.HIGHEST)


def _split(v):
    hi = v.astype(BF16)
    lo = (v - hi.astype(F32)).astype(BF16)
    return hi, lo


def _tri_dot(t, v):
    hi, lo = _split(v)
    return _dot(t, hi) + _dot(t, lo)


def _dot_tri(v, t):
    hi, lo = _split(v)
    return _dot(hi, t) + _dot(lo, t)


def _tri_consts(q):
    ri = lax.broadcasted_iota(jnp.int32, (q, q), 0)
    ci = lax.broadcasted_iota(jnp.int32, (q, q), 1)
    lower = ri >= ci
    upper = ci >= ri
    t_lower = jnp.where(lower, 1.0, 0.0).astype(BF16)
    t_upper = jnp.where(upper, 1.0, 0.0).astype(BF16)
    return (lower, upper), (t_lower, t_upper)


def _shift_rows(x, k):
    if k == 0:
        return x
    n = x.shape[0]
    rows = lax.broadcasted_iota(jnp.int32, x.shape, 0)
    y = pltpu.roll(x, k % n, axis=0)
    if k > 0:
        return jnp.where(rows >= k, y, 0.0)
    return jnp.where(rows < n + k, y, 0.0)


def _dwconv(x, w, b):
    taps = w.shape[0]
    pad = (taps - 1) // 2
    acc = x * w[pad:pad + 1, :] + b
    for j in range(taps):
        if j != pad:
            acc = acc + _shift_rows(x, pad - j) * w[j:j + 1, :]
    return acc


def _local_unroll(nc, limit=8):
    return next(u for u in (8, 4, 2, 1) if u <= limit and nc % u == 0)


CONV_HALO = 8


def _conv_blocks(src_ref, pad_ref, w, b, rb, emit):
    seq = src_ref.shape[0]
    taps = w.shape[0]
    pad = (taps - 1) // 2
    assert pad < CONV_HALO
    zeros = jnp.zeros((CONV_HALO, pad_ref.shape[1]), F32)
    pad_ref[0:CONV_HALO, :] = zeros
    pad_ref[CONV_HALO + seq:CONV_HALO + seq + CONV_HALO, :] = zeros
    pad_ref[CONV_HALO:CONV_HALO + seq, :] = src_ref[...].astype(F32)
    n = rb + 2 * CONV_HALO
    for r0 in range(0, seq, rb):
        win = pad_ref[r0:r0 + n, :]
        acc = b
        for j in range(taps):
            shift = pad - j
            moved = win if shift == 0 else pltpu.roll(win, shift % n, axis=0)
            acc = acc + moved[CONV_HALO:CONV_HALO + rb, :] * w[j:j + 1, :]
        emit(r0, acc)


def _rms(x):
    return x * lax.rsqrt(jnp.mean(x * x, axis=-1, keepdims=True) + EPS)


def _inproj_kernel(*refs, tn, with_gates):
    if with_gates:
        x_ref, g_ref, w_ref, wg_ref, o_ref, og_ref = refs
    else:
        x_ref, g_ref, w_ref, o_ref = refs
    xn = (_rms(x_ref[...]) * g_ref[...]).astype(BF16)
    for c0 in range(0, w_ref.shape[1], tn):
        o_ref[:, c0:c0 + tn] = _dot(xn, w_ref[:, c0:c0 + tn]).astype(o_ref.dtype)
    if with_gates:
        og_ref[...] = _dot(xn, wg_ref[...])


def _inproj(x2d, g, w, w_gate, tn):
    m, k = x2d.shape
    n = w.shape[1]
    tm = min(INPROJ_TM, m)
    with_gates = w_gate is not None
    in_specs = [pl.BlockSpec((tm, k), lambda i: (i, 0)),
                pl.BlockSpec((1, k), lambda i: (0, 0)),
                pl.BlockSpec((k, n), lambda i: (0, 0), pipeline_mode=pl.Buffered(1))]
    out_specs = [pl.BlockSpec((tm, n), lambda i: (i, 0))]
    out_shape = [jax.ShapeDtypeStruct((m, n), BF16)]
    args = [x2d, g.reshape(1, k).astype(F32), w]
    if with_gates:
        in_specs.append(pl.BlockSpec((k, LANES), lambda i: (0, 0), pipeline_mode=pl.Buffered(1)))
        out_specs.append(pl.BlockSpec((tm, LANES), lambda i: (i, 0)))
        out_shape.append(jax.ShapeDtypeStruct((m, LANES), F32))
        args.append(w_gate)
    outs = pl.pallas_call(
        functools.partial(_inproj_kernel, tn=tn, with_gates=with_gates),
        grid=(m // tm,),
        in_specs=in_specs,
        out_specs=out_specs,
        out_shape=out_shape,
        compiler_params=_params("parallel"),
        name="inproj",
    )(*args)
    return (outs[0], outs[1]) if with_gates else (outs[0], None)


def _outproj_kernel(y_ref, w_ref, r_ref, g_ref, o_ref, *, final):
    acc = _dot(y_ref[...], w_ref[...]) + r_ref[...]
    if final:
        acc = _rms(acc) * g_ref[...]
    o_ref[...] = acc


def _outproj(y2d, w, res2d, final_g=None):
    m, k = y2d.shape
    n = w.shape[1]
    tm = min(512, m)
    final = final_g is not None
    g = (final_g if final else jnp.ones((n,), F32)).reshape(1, n).astype(F32)
    return pl.pallas_call(
        functools.partial(_outproj_kernel, final=final),
        grid=(m // tm,),
        in_specs=[pl.BlockSpec((tm, k), lambda i: (i, 0)),
                  pl.BlockSpec((k, n), lambda i: (0, 0)),
                  pl.BlockSpec((tm, n), lambda i: (i, 0)),
                  pl.BlockSpec((1, n), lambda i: (0, 0))],
        out_specs=pl.BlockSpec((tm, n), lambda i: (i, 0)),
        out_shape=jax.ShapeDtypeStruct((m, n), F32),
        compiler_params=_params("parallel"),
        name="outproj",
    )(y2d, w.astype(BF16), res2d, g)


def _pad_cols(w, n):
    return jnp.pad(w, ((0, 0), (0, n - w.shape[1])))


def _ssd_kernel(z_ref, x_ref, b_ref, c_ref, dtc_ref, dtr_ref,
                cwx_ref, cwb_ref, cwc_ref, cbx_ref, cbb_ref, cbc_ref,
                biasr_ref, biasc_ref, alogr_ref, alogc_ref, dskip_ref, gnorm_ref,
                o_ref, xs_ref, cs_ref, bt_ref, y_ref, s_ref, u_ref, e1_ref, dec_ref, padx_ref, padn_ref):
    q = SSD_CHUNK
    seq = x_ref.shape[0]
    nc = seq // q
    hpg = SSD_HPG
    assert hpg * SSD_HEAD_DIM == 2 * LANES
    (lower, upper), (t_lower, t_upper) = _tri_consts(q)
    lane_head = lax.shift_right_logical(
        lax.broadcasted_iota(jnp.int32, (1, SSD_GW), 1), int(math.log2(SSD_HEAD_DIM)))
    first_head = lax.broadcasted_iota(jnp.int32, (1, LANES), 1) < SSD_HEAD_DIM
    fwd_lane = lax.broadcasted_iota(jnp.int32, (1, 2 * hpg), 1) < hpg
    fwd_row = lax.broadcasted_iota(jnp.int32, (2 * hpg, 1), 0) < hpg

    def emit_x(r0, blk):
        xs_ref[r0:r0 + q, :] = _silu(blk)

    def emit_c(r0, blk):
        cs_ref[r0:r0 + q, :] = _silu(blk).astype(BF16)

    def emit_b(r0, blk):
        bt_ref[r0 // q] = _silu(blk).T.astype(BF16)

    _conv_blocks(x_ref, padx_ref, cwx_ref[...], cbx_ref[...], q, emit_x)
    _conv_blocks(c_ref, padn_ref, cwc_ref[...], cbc_ref[...], q, emit_c)
    _conv_blocks(b_ref, padn_ref, cwb_ref[...], cbb_ref[...], q, emit_b)

    bias_row = biasr_ref[...]
    bias_col = biasc_ref[...]
    a_row = -jnp.exp(alogr_ref[...])
    a_col = -jnp.exp(alogc_ref[...])
    dskip = dskip_ref[...]

    def rep(cols, j):
        return jnp.broadcast_to(cols[:, j:j + 1], (cols.shape[0], LANES))

    sel_dt = jnp.where(
        lax.broadcasted_iota(jnp.int32, (2 * hpg, 2 * SSD_GW), 0)
        == lax.shift_right_logical(lax.broadcasted_iota(jnp.int32, (2 * hpg, 2 * SSD_GW), 1),
                                   int(math.log2(SSD_HEAD_DIM))), 1.0, 0.0).astype(BF16)

    unroll = _local_unroll(nc)

    def local_body(i, carry):
        chunks = [i * unroll + j for j in range(unroll)]
        rows = [pl.ds(pl.multiple_of(c * q, q), q) for c in chunks]
        us = range(unroll)
        dt_col = [_softplus(dtc_ref[r, :] + bias_row) for r in rows]
        dt_row = [_softplus(dtr_ref[c] + bias_col) for c in chunks]
        la_col = [dt * a_row for dt in dt_col]
        la_row = [dt * a_col for dt in dt_row]
        pre_col = [_tri_dot(t_lower, la) for la in la_col]
        pre_row = [_dot_tri(la, t_upper) for la in la_row]
        cbs = [_dot(cs_ref[rows[j], :], bt_ref[chunks[j]]) for j in us]
        dt_exp = [_dot(dt.astype(BF16), sel_dt) for dt in dt_col]
        ys, xqs, cum_exps = [], [], []
        for j in us:
            cum_col = jnp.where(fwd_lane, pre_col[j], pre_col[j][q - 1:q, :] - pre_col[j] + la_col[j])
            cum_row = jnp.where(fwd_row, pre_row[j], pre_row[j][:, q - 1:q] - pre_row[j] + la_row[j])
            reps = [rep(cum_col, r) for r in range(2 * hpg)]
            ms = []
            for r in range(hpg):
                rb = hpg + r
                mf = jnp.exp(jnp.where(lower, reps[r] - cum_row[r:r + 1, :], _NEG_INF)) * dt_row[j][r:r + 1, :]
                mb = jnp.exp(jnp.where(upper, reps[rb] - cum_row[rb:rb + 1, :], _NEG_INF)) * dt_row[j][rb:rb + 1, :]
                ms.append((cbs[j] * (mf + mb)).astype(BF16))
            xq = xs_ref[rows[j], :]
            xb = xq.astype(BF16)
            xbd = jnp.concatenate(
                [jnp.where(lane_head == r, xb, jnp.zeros_like(xb)) for r in range(hpg)], axis=0)
            ys.append(_dot(jnp.concatenate(ms, axis=1), xbd))
            xqs.append(xq)
            cum_exps.append([jnp.concatenate(
                [jnp.where(first_head, reps[lo], reps[lo + 1]),
                 jnp.where(first_head, reps[lo + 2], reps[lo + 3])], axis=1) for lo in (0, hpg)])
        for j in us:
            y_ref[rows[j], :] = ys[j] + xqs[j] * dskip
            for d in range(2):
                cum_exp = cum_exps[j][d]
                tot = cum_exp[q - 1:q, :] if d == 0 else cum_exp[0:1, :]
                w = dt_exp[j][:, d * SSD_GW:(d + 1) * SSD_GW] * jnp.exp(tot - cum_exp)
                u_ref[d, chunks[j]] = _dot(bt_ref[chunks[j]], (xqs[j] * w).astype(BF16))
                e1_ref[d, rows[j], :] = jnp.exp(cum_exp).astype(BF16)
                dec_ref[d, chunks[j]] = jnp.exp(tot)
        return carry

    lax.fori_loop(0, nc // unroll, local_body, 0)

    s_ref[...] = jnp.zeros_like(s_ref)

    def serial(i, carry):
        chunks = (i, nc - 1 - i)
        rows = [pl.ds(pl.multiple_of(c * q, q), q) for c in chunks]
        states = [s_ref[d] for d in range(2)]
        inter = [_dot(cs_ref[rows[d], :], states[d].astype(BF16)) for d in range(2)]
        for d in range(2):
            s_ref[d] = states[d] * dec_ref[d, chunks[d]] + u_ref[d, chunks[d]]
        for d in range(2):
            y_ref[rows[d], :] += inter[d] * e1_ref[d, rows[d], :].astype(F32)
        return carry

    lax.fori_loop(0, nc, serial, 0)

    y = y_ref[...] * _silu(z_ref[...].astype(F32))
    o_ref[...] = (_rms(y) * gnorm_ref[...]).astype(o_ref.dtype)


def _ssd_layer(h, norm, w_in, conv_w, conv_b, dt_bias, a_log, d_skip, gnorm, w_out, final_g):
    bsz, seq, _ = h.shape
    g, hpg, q = SSD_GROUPS, SSD_HPG, SSD_CHUNK
    nc = seq // q
    h2 = h.reshape(bsz * seq, D_MODEL)
    n_main = 2 * D_INNER + 2 * SSD_BC
    pm, gates = _inproj(h2, norm, w_in[:, :n_main].astype(BF16),
                        _pad_cols(w_in[:, n_main:], LANES).astype(BF16), INPROJ_TN)
    pm = pm.reshape(bsz, seq, n_main)
    dt = gates[:, :2 * SSD_HEADS].reshape(bsz, seq, 2, g, hpg)
    dt_col = dt.transpose(0, 3, 1, 2, 4).reshape(bsz, g, seq, 2 * hpg)
    dt_row = dt.reshape(bsz, nc, q, 2, g, hpg).transpose(0, 4, 1, 3, 5, 2).reshape(bsz, g, nc, 2 * hpg, q)

    def per_group(p):
        return p.astype(F32).reshape(2, g, hpg).transpose(1, 0, 2).reshape(g, 2 * hpg)

    bias, alog = per_group(dt_bias), per_group(a_log)
    dskip = jnp.repeat(d_skip.astype(F32).reshape(g, hpg), SSD_HEAD_DIM, axis=1).reshape(g, 1, SSD_GW)
    cw = conv_w.astype(F32)
    cb = conv_b.astype(F32).reshape(1, -1)
    taps = cw.shape[0]
    xo = D_INNER // SSD_GW
    bo = 2 * D_INNER // SSD_STATE
    co = bo + SSD_GROUPS
    cbo = D_INNER // SSD_STATE
    cco = cbo + SSD_GROUPS

    y = pl.pallas_call(
        _ssd_kernel,
        grid=(bsz, g),
        in_specs=[
            pl.BlockSpec((None, seq, SSD_GW), lambda b, j: (b, 0, j)),
            pl.BlockSpec((None, seq, SSD_GW), lambda b, j: (b, 0, xo + j)),
            pl.BlockSpec((None, seq, SSD_STATE), lambda b, j: (b, 0, bo + j)),
            pl.BlockSpec((None, seq, SSD_STATE), lambda b, j: (b, 0, co + j)),
            pl.BlockSpec((None, None, seq, 2 * hpg), lambda b, j: (b, j, 0, 0)),
            pl.BlockSpec((None, None, nc, 2 * hpg, q), lambda b, j: (b, j, 0, 0, 0)),
            pl.BlockSpec((taps, SSD_GW), lambda b, j: (0, j)),
            pl.BlockSpec((taps, SSD_STATE), lambda b, j: (0, cbo + j)),
            pl.BlockSpec((taps, SSD_STATE), lambda b, j: (0, cco + j)),
            pl.BlockSpec((1, SSD_GW), lambda b, j: (0, j)),
            pl.BlockSpec((1, SSD_STATE), lambda b, j: (0, cbo + j)),
            pl.BlockSpec((1, SSD_STATE), lambda b, j: (0, cco + j)),
            pl.BlockSpec((None, 1, 2 * hpg), lambda b, j: (j, 0, 0)),
            pl.BlockSpec((None, 2 * hpg, 1), lambda b, j: (j, 0, 0)),
            pl.BlockSpec((None, 1, 2 * hpg), lambda b, j: (j, 0, 0)),
            pl.BlockSpec((None, 2 * hpg, 1), lambda b, j: (j, 0, 0)),
            pl.BlockSpec((None, 1, SSD_GW), lambda b, j: (j, 0, 0)),
            pl.BlockSpec((None, 1, SSD_GW), lambda b, j: (j, 0, 0)),
        ],
        out_specs=pl.BlockSpec((None, seq, SSD_GW), lambda b, j: (b, 0, j)),
        out_shape=jax.ShapeDtypeStruct((bsz, seq, D_INNER), BF16),
        scratch_shapes=[pltpu.VMEM((seq, SSD_GW), F32),
                        pltpu.VMEM((seq, SSD_STATE), BF16),
                        pltpu.VMEM((nc, SSD_STATE, q), BF16),
                        pltpu.VMEM((seq, SSD_GW), F32),
                        pltpu.VMEM((2, SSD_STATE, SSD_GW), F32),
                        pltpu.VMEM((2, nc, SSD_STATE, SSD_GW), F32),
                        pltpu.VMEM((2, seq, SSD_GW), BF16),
                        pltpu.VMEM((2, nc, 1, SSD_GW), F32),
                        pltpu.VMEM((seq + 2 * CONV_HALO, SSD_GW), F32),
                        pltpu.VMEM((seq + 2 * CONV_HALO, SSD_STATE), F32)],
        compiler_params=_params("parallel", "parallel"),
        name="ssd_mixer",
    )(pm, pm, pm, pm, dt_col, dt_row, cw, cw, cw, cb, cb, cb,
      bias.reshape(g, 1, 2 * hpg), bias.reshape(g, 2 * hpg, 1),
      alog.reshape(g, 1, 2 * hpg), alog.reshape(g, 2 * hpg, 1),
      dskip, gnorm.astype(F32).reshape(g, 1, SSD_GW))
    out = _outproj(y.reshape(bsz * seq, D_INNER), w_out, h2, final_g)
    return out.reshape(bsz, seq, D_MODEL)


def _gla_kernel(q_ref, k_ref, v_ref, z_ref, gl_ref, wg_ref, bg_ref, onorm_ref,
                o_ref, lg_ref, acc_ref, s_ref, qg_ref, u_ref, dec_ref):
    q = GLA_CHUNK
    seq = q_ref.shape[0]
    nc = seq // q
    masks, tris = _tri_consts(q)
    gl = gl_ref[...]
    for d in range(2):
        lg_ref[d] = _log_sigmoid(_dot_tri(gl, wg_ref[d]) + bg_ref[d]) * (1.0 / GLA_NORMALIZER)

    unroll = _local_unroll(nc)

    def local_body(i, carry):
        chunks = [i * unroll + j for j in range(unroll)]
        rows = [pl.ds(pl.multiple_of(c * q, q), q) for c in chunks]
        pairs = [(j, d) for j in range(unroll) for d in range(2)]
        cums = {(j, d): _tri_dot(tris[d], lg_ref[d, rows[j], :]) for j, d in pairs}
        qgs, kgs, kds = {}, {}, {}
        for j, d in pairs:
            cum = cums[j, d]
            qc = q_ref[rows[j], :].astype(F32) * (GLA_DK ** -0.5)
            kc = k_ref[rows[j], :].astype(F32)
            tot = cum[q - 1:q, :] if d == 0 else cum[0:1, :]
            qgs[j, d] = (qc * jnp.exp(cum)).astype(BF16)
            kgs[j, d] = (kc * jnp.exp(-cum)).astype(BF16)
            kds[j, d] = (kc * jnp.exp(tot - cum)).astype(BF16)
            qg_ref[d, rows[j], :] = qgs[j, d]
            dec_ref[d, chunks[j]] = jnp.exp(tot)
        atts = {p: _dot_nt(qgs[p], kgs[p]) for p in pairs}
        for j, d in pairs:
            u_ref[d, chunks[j]] = _dot_tn(v_ref[rows[j], :], kds[j, d]).astype(BF16)
        atts = {(j, d): jnp.where(masks[d], atts[j, d], 0.0).astype(BF16) for j, d in pairs}
        outs = {(j, d): _dot(atts[j, d], v_ref[rows[j], :]) for j, d in pairs}
        for j in range(unroll):
            acc_ref[rows[j], :] = outs[j, 0] + outs[j, 1]
        return carry

    lax.fori_loop(0, nc // unroll, local_body, 0)

    s_ref[...] = jnp.zeros_like(s_ref)

    def serial(i, carry):
        chunks = (i, nc - 1 - i)
        rows = [pl.ds(pl.multiple_of(c * q, q), q) for c in chunks]
        states = [s_ref[d] for d in range(2)]
        inter = [_dot_nt(qg_ref[d, rows[d], :], states[d].astype(BF16)) for d in range(2)]
        for d in range(2):
            s_ref[d] = states[d] * dec_ref[d, chunks[d]] + u_ref[d, chunks[d]].astype(F32)
        for d in range(2):
            acc_ref[rows[d], :] += inter[d]
        return carry

    lax.fori_loop(0, nc, serial, 0)

    out = _rms(acc_ref[...]) * onorm_ref[...]
    o_ref[...] = (out * _silu(z_ref[...].astype(F32))).astype(o_ref.dtype)


def _gla_layer(h, norm, w_in, w_gate, b_gate, onorm, w_out, final_g):
    bsz, seq, _ = h.shape
    nh = GLA_HEADS
    h2 = h.reshape(bsz * seq, D_MODEL)
    n_main = 2 * GLA_QK + 2 * D_INNER
    pm, gates = _inproj(h2, norm, w_in[:, :n_main].astype(BF16),
                        _pad_cols(w_in[:, n_main:], LANES).astype(BF16), INPROJ_TN)
    pm = pm.reshape(bsz, seq, n_main)
    gates = gates.reshape(bsz, seq, LANES)
    wg = jnp.zeros((2, LANES, GLA_QK), F32)
    for d in range(2):
        wg = wg.at[d, d * GLA_RANK:(d + 1) * GLA_RANK, :].set(w_gate[d].astype(F32))
    vo = 2 * GLA_QK // GLA_DV
    zo = vo + nh
    y = pl.pallas_call(
        _gla_kernel,
        grid=(bsz, nh),
        in_specs=[
            pl.BlockSpec((None, seq, GLA_DK), lambda b, j: (b, 0, j)),
            pl.BlockSpec((None, seq, GLA_DK), lambda b, j: (b, 0, nh + j)),
            pl.BlockSpec((None, seq, GLA_DV), lambda b, j: (b, 0, vo + j)),
            pl.BlockSpec((None, seq, GLA_DV), lambda b, j: (b, 0, zo + j)),
            pl.BlockSpec((None, seq, LANES), lambda b, j: (b, 0, 0)),
            pl.BlockSpec((2, LANES, GLA_DK), lambda b, j: (0, 0, j)),
            pl.BlockSpec((2, 1, GLA_DK), lambda b, j: (0, 0, j)),
            pl.BlockSpec((1, GLA_DV), lambda b, j: (0, 0)),
        ],
        out_specs=pl.BlockSpec((None, seq, GLA_DV), lambda b, j: (b, 0, j)),
        out_shape=jax.ShapeDtypeStruct((bsz, seq, D_INNER), BF16),
        scratch_shapes=[pltpu.VMEM((2, seq, GLA_DK), F32),
                        pltpu.VMEM((seq, GLA_DV), F32),
                        pltpu.VMEM((2, GLA_DV, GLA_DK), F32),
                        pltpu.VMEM((2, seq, GLA_DK), BF16),
                        pltpu.VMEM((2, seq // GLA_CHUNK, GLA_DV, GLA_DK), BF16),
                        pltpu.VMEM((2, seq // GLA_CHUNK, 1, GLA_DK), F32)],
        compiler_params=_params("parallel", "parallel"),
        name="gla_mixer",
    )(pm, pm, pm, pm, gates, wg.astype(BF16), b_gate.astype(F32).reshape(2, 1, GLA_QK),
      onorm.astype(F32).reshape(1, GLA_DV))
    out = _outproj(y.reshape(bsz * seq, D_INNER), w_out, h2, final_g)
    return out.reshape(bsz, seq, D_MODEL)


def _hy_mlp_kernel(f_ref, w0_ref, b0_ref, w1_ref, b1_ref, w2_ref, b2_ref, fr_ref, o_ref):
    fr = fr_ref[...]
    hid = jnp.sin(fr[0:1, :] * (_dot_f32(f_ref[...], w0_ref[...]) + b0_ref[...]))
    hid = jnp.sin(fr[1:2, :] * (_dot_f32(hid, w1_ref[...]) + b1_ref[...]))
    hid = jnp.sin(fr[2:3, :] * (_dot_f32(hid, w2_ref[...]) + b2_ref[...]))
    o_ref[...] = hid


def _hy_spec_kernel(hid_ref, wf_ref, wb_ref, t_ref, dl_ref, cos_ref, sin_ref, cosr_ref, sinr_ref,
                    kre_ref, ks_ref, p_ref):
    lb = hid_ref.shape[0] // 2

    @pl.when(pl.program_id(2) == 0)
    def _():
        env = jnp.exp(-t_ref[...] * dl_ref[...])
        hf = _dot_f32(hid_ref[...], wf_ref[...]) * env
        hb = _dot_f32(hid_ref[...], wb_ref[...]) * env
        first = lax.broadcasted_iota(jnp.int32, (lb, hf.shape[1]), 0) == 0
        hf_lo0 = jnp.where(first, 0.0, hf[0:lb])
        hb_lo0 = jnp.where(first, 0.0, hb[0:lb])
        p_ref[0] = (hf[0:lb] + hb_lo0).astype(BF16)
        p_ref[1] = (hf[0:lb] - hb_lo0).astype(BF16)
        p_ref[2] = hf[lb:].astype(BF16)
        p_ref[3] = hf_lo0.astype(BF16)
        p_ref[4] = hb[lb:].astype(BF16)
        p_ref[5] = hb_lo0.astype(BF16)

    scale = 1.0 / lb
    cos, sin, cosr, sinr = cos_ref[...], sin_ref[...], cosr_ref[...], sinr_ref[...]
    dt = kre_ref.dtype
    kre_ref[0] = (_dot(cos, p_ref[0]) * scale).astype(dt)
    ks_ref[0] = (_dot(sin, p_ref[1]) * scale).astype(dt)
    kre_ref[1] = ((_dot(cos, p_ref[2]) + _dot(cosr, p_ref[3])) * scale).astype(dt)
    ks_ref[1] = ((_dot(sin, p_ref[2]) + _dot(sinr, p_ref[3])) * scale).astype(dt)
    kre_ref[2] = ((_dot(cos, p_ref[4]) + _dot(cosr, p_ref[5])) * scale).astype(dt)
    ks_ref[2] = ((_dot(sin, p_ref[4]) + _dot(sinr, p_ref[5])) * (-scale)).astype(dt)


def _hy_conv_kernel(v_ref, x1_ref, x2_ref, z_ref, cwv_ref, cw1_ref, cw2_ref,
                    cbv_ref, cb1_ref, cb2_ref, kre_ref, ks_ref, d_ref, w_ref, o_ref,
                    y_ref, yb_ref, zf_ref, g_ref, yf_ref, *, fb):
    seq, tc = v_ref.shape
    lb = seq // 2
    nblk = lb // fb
    sb = min(64, fb)
    y_ref[...] = _dwconv(v_ref[...].astype(F32), cwv_ref[...], cbv_ref[...])
    gates = ((x1_ref, cw1_ref, cb1_ref), (x2_ref, cw2_ref, cb2_ref))
    for o, (x_ref, cw_ref, cb_ref) in enumerate(gates):
        yb_ref[:, 0:tc] = y_ref[0:lb, :].astype(BF16)
        yb_ref[:, tc:2 * tc] = y_ref[lb:seq, :].astype(BF16)
        g_ref[...] = _dwconv(x_ref[...].astype(F32), cw_ref[...], cb_ref[...])

        yb = yb_ref[...]
        for k in range(nblk):
            f0 = k * fb
            yf_ref[k, 0] = _dot(w_ref[f0:f0 + fb, :], yb)
            yf_ref[k, 1] = _dot(w_ref[lb + f0:lb + f0 + fb, :], yb)
        for k in range(nblk):
            for s0 in range(0, fb, sb):
                f0 = k * fb + s0
                re = [yf_ref[k, 0, s0:s0 + sb, 0:tc], yf_ref[k, 0, s0:s0 + sb, tc:2 * tc]]
                im = [yf_ref[k, 1, s0:s0 + sb, 0:tc], yf_ref[k, 1, s0:s0 + sb, tc:2 * tc]]
                kre = [kre_ref[o, n, f0:f0 + sb, :].astype(F32) for n in range(3)]
                ks = [ks_ref[o, n, f0:f0 + sb, :].astype(F32) for n in range(3)]
                for i, (n0, n1) in enumerate(((0, 2), (1, 0))):
                    zre = re[0] * kre[n0] - im[0] * ks[n0] + re[1] * kre[n1] - im[1] * ks[n1]
                    zim = im[0] * kre[n0] + re[0] * ks[n0] + im[1] * kre[n1] + re[1] * ks[n1]
                    zf_ref[f0:f0 + sb, i * tc:(i + 1) * tc] = zre.astype(BF16)
                    zf_ref[lb + f0:lb + f0 + sb, i * tc:(i + 1) * tc] = zim.astype(BF16)

        for k in range(nblk):
            t0 = k * fb
            conv = (_dot(w_ref[t0:t0 + fb, :], zf_ref[0:lb, :])
                    + _dot(w_ref[lb + t0:lb + t0 + fb, :], zf_ref[lb:2 * lb, :]))
            for i in range(2):
                r0 = i * lb + t0
                y_ref[r0:r0 + fb, :] = g_ref[r0:r0 + fb, :] * (conv[:, i * tc:(i + 1) * tc]
                                                               + y_ref[r0:r0 + fb, :] * d_ref[o:o + 1, :])
    o_ref[...] = (y_ref[...] * _silu(z_ref[...].astype(F32))).astype(o_ref.dtype)


def _hyena_layer(h, norm, w_in, conv_w, conv_b, ffn_w_in, ffn_b_in, ffn_w_hid, ffn_b_hid,
                 ffn_freq, ffn_w_out, d_bias, w_out, final_g):
    bsz, seq, _ = h.shape
    c = D_INNER
    h2 = h.reshape(bsz * seq, D_MODEL)
    pm = _inproj(h2, norm, w_in.astype(BF16), None, INPROJ_TN)[0].reshape(bsz, seq, 4 * c)

    t = jnp.linspace(0.0, 1.0, seq, dtype=F32)[:, None]
    pos = jnp.arange(seq, dtype=F32)[:, None]
    bands = jnp.linspace(1e-4, HY_BANDS - 1, HY_BANDS, dtype=F32)[None]
    ang = (2.0 * math.pi / seq) * pos * bands
    feats = jnp.concatenate([t, jnp.cos(ang), -jnp.sin(ang)], axis=-1)
    feats = _pad_cols(feats, LANES)
    w0 = jnp.pad(ffn_w_in.astype(F32), ((0, LANES - HY_EMB), (0, 0)))
    hid = pl.pallas_call(
        _hy_mlp_kernel,
        out_shape=jax.ShapeDtypeStruct((seq, HY_FFN), F32),
        name="hyena_filter_mlp",
    )(feats, w0, ffn_b_in.astype(F32).reshape(1, HY_FFN),
      ffn_w_hid[0].astype(F32), ffn_b_hid[0].astype(F32).reshape(1, HY_FFN),
      ffn_w_hid[1].astype(F32), ffn_b_hid[1].astype(F32).reshape(1, HY_FFN),
      ffn_freq.astype(F32))

    max_decay = math.log(HY_TARGET) / HY_FAST_PCT
    min_decay = math.log(HY_TARGET) / HY_SLOW_PCT
    deltas = jnp.abs(jnp.linspace(min_decay, max_decay, c, dtype=F32)).reshape(1, c)

    lb = seq // 2
    fi = jnp.arange(lb, dtype=jnp.int32)
    n4 = 8 * lb

    def table(num, fn):
        return fn((2.0 * math.pi / n4) * (num % n4).astype(F32)).astype(BF16)

    num1 = (2 * fi[:, None] + 1) * (2 * fi[None, :])
    numr = (2 * fi[:, None] + 1) * (2 * (fi[None, :] - lb))
    num2 = (2 * fi[:, None] + 1) * (2 * fi[None, :] + 1)
    cos1, sin1 = table(num1, jnp.cos), table(num1, jnp.sin)
    cosr, sinr = table(numr, jnp.cos), table(numr, jnp.sin)
    w2 = jnp.concatenate([table(num2, jnp.cos), table(num2, jnp.sin)], axis=0)

    tc = min(512, c)
    fb = min(512, lb)
    nct = c // tc
    wo = ffn_w_out.astype(F32)
    tab_spec = pl.BlockSpec((fb, lb), lambda o, j, k: (k, 0))
    kre, ks = pl.pallas_call(
        _hy_spec_kernel,
        grid=(HY_ORDER, nct, lb // fb),
        in_specs=[
            pl.BlockSpec((seq, HY_FFN), lambda o, j, k: (0, 0)),
            pl.BlockSpec((HY_FFN, tc), lambda o, j, k: (0, o * nct + j)),
            pl.BlockSpec((HY_FFN, tc), lambda o, j, k: (0, (HY_ORDER + o) * nct + j)),
            pl.BlockSpec((seq, 1), lambda o, j, k: (0, 0)),
            pl.BlockSpec((1, tc), lambda o, j, k: (0, j)),
            tab_spec, tab_spec, tab_spec, tab_spec,
        ],
        out_specs=[pl.BlockSpec((None, 3, fb, tc), lambda o, j, k: (o, 0, k, j)),
                   pl.BlockSpec((None, 3, fb, tc), lambda o, j, k: (o, 0, k, j))],
        out_shape=[jax.ShapeDtypeStruct((HY_ORDER, 3, lb, c), BF16)] * 2,
        scratch_shapes=[pltpu.VMEM((6, lb, tc), BF16)],
        compiler_params=_params("parallel", "parallel", "arbitrary"),
        name="hyena_filter_spectrum",
    )(hid, wo, wo, t, deltas, cos1, sin1, cosr, sinr)

    tcc = min(256, c)
    ncc = c // tcc
    cw = conv_w.astype(F32)
    cb = conv_b.astype(F32).reshape(1, -1)
    taps = cw.shape[0]
    y = pl.pallas_call(
        functools.partial(_hy_conv_kernel, fb=fb),
        grid=(ncc, bsz),
        in_specs=[
            pl.BlockSpec((None, seq, tcc), lambda j, b: (b, 0, j)),
            pl.BlockSpec((None, seq, tcc), lambda j, b: (b, 0, ncc + j)),
            pl.BlockSpec((None, seq, tcc), lambda j, b: (b, 0, 2 * ncc + j)),
            pl.BlockSpec((None, seq, tcc), lambda j, b: (b, 0, 3 * ncc + j)),
            pl.BlockSpec((taps, tcc), lambda j, b: (0, j)),
            pl.BlockSpec((taps, tcc), lambda j, b: (0, ncc + j)),
            pl.BlockSpec((taps, tcc), lambda j, b: (0, 2 * ncc + j)),
            pl.BlockSpec((1, tcc), lambda j, b: (0, j)),
            pl.BlockSpec((1, tcc), lambda j, b: (0, ncc + j)),
            pl.BlockSpec((1, tcc), lambda j, b: (0, 2 * ncc + j)),
            pl.BlockSpec((HY_ORDER, 3, lb, tcc), lambda j, b: (0, 0, 0, j), pipeline_mode=pl.Buffered(1)),
            pl.BlockSpec((HY_ORDER, 3, lb, tcc), lambda j, b: (0, 0, 0, j), pipeline_mode=pl.Buffered(1)),
            pl.BlockSpec((HY_ORDER, tcc), lambda j, b: (0, j)),
            pl.BlockSpec((2 * lb, lb), lambda j, b: (0, 0), pipeline_mode=pl.Buffered(1)),
        ],
        out_specs=pl.BlockSpec((None, seq, tcc), lambda j, b: (b, 0, j)),
        out_shape=jax.ShapeDtypeStruct((bsz, seq, c), BF16),
        scratch_shapes=[pltpu.VMEM((seq, tcc), F32),
                        pltpu.VMEM((lb, 2 * tcc), BF16),
                        pltpu.VMEM((2 * lb, 2 * tcc), BF16),
                        pltpu.VMEM((seq, tcc), F32),
                        pltpu.VMEM((lb // fb, 2, fb, 2 * tcc), F32)],
        compiler_params=_params("parallel", "parallel"),
        name="hyena_mixer",
    )(pm, pm, pm, pm, cw, cw, cw, cb, cb, cb, kre, ks, d_bias.astype(F32), w2)
    out = _outproj(y.reshape(bsz * seq, c), w_out, h2, final_g)
    return out.reshape(bsz, seq, D_MODEL)


def _mlstm_kernel(xm_ref, z_ref, og_ref, gc_ref, gr_ref, cw_ref, cb_ref, wq_ref, wk_ref, wv_ref,
                  gbr_ref, gbc_ref, skip_ref, onorm_ref,
                  o_ref, q_ref, k_ref, v_ref, ch_ref, acc_ref, c_ref, nl_ref, strep_ref, kt_ref, str_ref, cmat_ref,
                  par_ref, sc_ref, rowc_ref, chunkc_ref):
    q = ML_CHUNK
    seq = xm_ref.shape[0]
    nc = seq // q
    masks, tris = _tri_consts(q)

    for j in range(ML_DH // LANES):
        sl = slice(j * LANES, (j + 1) * LANES)
        ch_ref[:, sl] = _silu(_dwconv(xm_ref[:, sl].astype(F32), cw_ref[:, sl], cb_ref[:, sl])).astype(BF16)
    ch = ch_ref[...]
    q_ref[...] = _dot(ch, wq_ref[...]).astype(BF16)
    kf = _dot(ch, wk_ref[...]) * (ML_DK ** -0.5)
    k_ref[...] = kf.astype(BF16)
    for c in range(nc):
        kt_ref[c] = kf[c * q:(c + 1) * q, :].T.astype(BF16)
    v_ref[...] =_dot(xm_ref[...], wv_ref[...]).astype(BF16)

    gb_row = gbr_ref[...]
    gb_col = gbc_ref[...]

    (t_lower, t_upper) = tris

    unroll = _local_unroll(nc, limit=4)

    def local_body(i, carry):
        chunks = [i * unroll + j for j in range(unroll)]
        rows = [pl.ds(pl.multiple_of(c * q, q), q) for c in chunks]
        pairs = [(j, d) for j in range(unroll) for d in range(2)]
        g_col = [gc_ref[r, :] + gb_row for r in rows]
        g_row = [gr_ref[c] + gb_col for c in chunks]
        lf_col = [_log_sigmoid(g) for g in g_col]
        lf_row = [_log_sigmoid(g) for g in g_row]
        tri_c, tri_r = (t_lower, t_upper), (t_upper, t_lower)
        b_cols = {(j, d): _tri_dot(tri_c[d], lf_col[j]) for j, d in pairs}
        b_rows = {(j, d): _dot_tri(lf_row[j], tri_r[d]) for j, d in pairs}
        qk = [_dot_nt(q_ref[r, :], k_ref[r, :]) for r in rows]
        ss = {}
        for j, d in pairs:
            i_col = g_col[j][:, 2 * d:2 * d + 1]
            i_row = g_row[j][2 * d:2 * d + 1, :]
            b_col = b_cols[j, d][:, 2 * d + 1:2 * d + 2]
            b_row = b_rows[j, d][2 * d + 1:2 * d + 2, :]
            logd = jnp.where(masks[d], b_col - b_row + i_row, _NEG_INF)
            m_loc = jnp.max(logd, axis=1, keepdims=True)
            s = qk[j] * jnp.exp(logd - m_loc)
            ss[j, d] = s.astype(BF16)
            den_loc = jnp.sum(s, axis=1, keepdims=True)
            tot = b_col[q - 1:q, :] if d == 0 else b_col[0:1, :]
            lw_col = tot - b_col + i_col
            lw_max = jnp.max(lw_col, axis=0, keepdims=True)
            for n, stat in enumerate((m_loc, den_loc, b_col)):
                strep_ref[d, n, rows[j], :] = jnp.broadcast_to(stat, (q, LANES))
            sc_ref[d, chunks[j], 0:1, :] = jnp.broadcast_to(tot, (1, LANES))
            sc_ref[d, chunks[j], 1:2, :] = jnp.broadcast_to(lw_max, (1, LANES))
            str_ref[d, chunks[j]] = jnp.broadcast_to(tot - b_row + i_row - lw_max, (8, q))
        for j, d in pairs:
            nl_ref[d, rows[j], :] = _dot(ss[j, d], v_ref[rows[j], :]).astype(BF16)
        return carry

    lax.fori_loop(0, nc // unroll, local_body, 0)

    c_ref[...] = jnp.zeros_like(c_ref)
    cmat_ref[...] = jnp.zeros_like(cmat_ref)
    acc_ref[...] = jnp.zeros_like(acc_ref)

    nblk = 4
    cb, ab = ML_DK // nblk, q // nblk

    def tile(x, n):
        return jnp.concatenate([x] * n, axis=1)

    def serial(i, carry):
        chunks = (i, nc - 1 - i)
        rows = [pl.ds(pl.multiple_of(c * q, q), q) for c in chunks]
        w_rows = [jnp.exp(str_ref[d, chunks[d]]).astype(BF16) for d in range(2)]
        for d in range(2):
            par_ref[d, q:q + ML_DK, :] = _dot(kt_ref[chunks[d]] * w_rows[d][0:1, :], v_ref[rows[d], :])
        for d in range(2):
            par_ref[d, 0:q, :] = _dot(q_ref[rows[d], :], cmat_ref[d])
        new = []
        for d in range(2):
            n_row, m = carry[d]
            ksum = _dot(w_rows[d], k_ref[rows[d], :])[0:1, :]
            qn = _dot_nt(q_ref[rows[d], :],
                         jnp.broadcast_to(n_row, (LANES, ML_DK)).astype(BF16))
            m_loc = strep_ref[d, 0, rows[d], :]
            den_loc = strep_ref[d, 1, rows[d], :]
            inter = strep_ref[d, 2, rows[d], :] + m
            m_row = jnp.maximum(m_loc, inter)
            a_loc = jnp.exp(m_loc - m_row)
            a_int = jnp.exp(inter - m_row)
            den = den_loc * a_loc + qn * a_int
            scale = 1.0 / jnp.maximum(jnp.abs(den), jnp.exp(-m_row))
            rowc_ref[d, 0] = a_loc * scale
            rowc_ref[d, 1] = a_int * scale
            tot = sc_ref[d, chunks[d], 0:1, :]
            lw_max = sc_ref[d, chunks[d], 1:2, :]
            m_new = jnp.maximum(tot + m, lw_max)
            dec = jnp.exp(tot + m - m_new)
            gain = jnp.exp(lw_max - m_new)
            chunkc_ref[d, 0:1, :] = tile(dec, ML_DH // LANES)
            chunkc_ref[d, 1:2, :] = tile(gain, ML_DH // LANES)
            new.append((n_row * tile(dec, ML_DK // LANES) + ksum * tile(gain, ML_DK // LANES), m_new))

        def update(b, inner):
            for d in range(2):
                r = pl.ds(pl.multiple_of(b * cb, cb), cb)
                cnew = (c_ref[d, r, :] * chunkc_ref[d, 0:1, :]
                        + par_ref[d, pl.ds(pl.multiple_of(q + b * cb, cb), cb), :] * chunkc_ref[d, 1:2, :])
                c_ref[d, r, :] = cnew
                cmat_ref[d, r, :] = cnew.astype(BF16)
                rl = pl.ds(pl.multiple_of(b * ab, ab), ab)
                ra = pl.ds(pl.multiple_of(chunks[d] * q + b * ab, ab), ab)
                acc_ref[ra, :] += (nl_ref[d, ra, :].astype(F32) * tile(rowc_ref[d, 0, rl, :], ML_DH // LANES)
                                   + par_ref[d, rl, :] * tile(rowc_ref[d, 1, rl, :], ML_DH // LANES))
            return inner

        lax.fori_loop(0, nblk, update, 0)
        return tuple(new)

    init = (jnp.zeros((1, ML_DK), F32), jnp.full((1, LANES), _NEG_INF, F32))
    lax.fori_loop(0, nc, serial, (init, init))

    hh = _sigmoid(og_ref[...].astype(F32)) * acc_ref[...]
    hh = _rms(hh) * onorm_ref[...] + skip_ref[...] * ch_ref[...].astype(F32)
    o_ref[...] = (hh * _silu(z_ref[...].astype(F32))).astype(o_ref.dtype)


def _mlstm_layer(h, norm, w_in, conv_w, conv_b, w_q, w_k, w_v, gate_b, skip, onorm, w_out, final_g):
    bsz, seq, _ = h.shape
    nh, q = ML_HEADS, ML_CHUNK
    nc = seq // q
    h2 = h.reshape(bsz * seq, D_MODEL)
    n_main = 3 * D_INNER
    pm, gates = _inproj(h2, norm, w_in[:, :n_main].astype(BF16),
                        _pad_cols(w_in[:, n_main:], LANES).astype(BF16), INPROJ_TN)
    pm = pm.reshape(bsz, seq, n_main)
    gt = gates[:, :4 * nh].reshape(bsz, seq, 2, 2, nh)
    g_col = gt.transpose(0, 4, 1, 2, 3).reshape(bsz, nh, seq, 4)
    g_row = gt.reshape(bsz, nc, q, 2, 2, nh).transpose(0, 5, 1, 3, 4, 2).reshape(bsz, nh, nc, 4, q)
    g_row = jnp.pad(g_row, ((0, 0), (0, 0), (0, 0), (0, 4), (0, 0)))
    gb =gate_b.astype(F32).transpose(2, 0, 1).reshape(nh, 4)
    taps = conv_w.shape[0]
    y = pl.pallas_call(
        _mlstm_kernel,
        grid=(bsz, nh),
        in_specs=[
            pl.BlockSpec((None, seq, ML_DH), lambda b, j: (b, 0, j)),
            pl.BlockSpec((None, seq, ML_DH), lambda b, j: (b, 0, nh + j)),
            pl.BlockSpec((None, seq, ML_DH), lambda b, j: (b, 0, 2 * nh + j)),
            pl.BlockSpec((None, None, seq, 4), lambda b, j: (b, j, 0, 0)),
            pl.BlockSpec((None, None, nc, 8, q), lambda b, j: (b, j, 0, 0, 0)),
            pl.BlockSpec((taps, ML_DH), lambda b, j: (0, j)),
            pl.BlockSpec((1, ML_DH), lambda b, j: (0, j)),
            pl.BlockSpec((None, ML_DH, ML_DK), lambda b, j: (j, 0, 0)),
            pl.BlockSpec((None, ML_DH, ML_DK), lambda b, j: (j, 0, 0)),
            pl.BlockSpec((None, ML_DH, ML_DH), lambda b, j: (j, 0, 0)),
            pl.BlockSpec((None, 1, 4), lambda b, j: (j, 0, 0)),
            pl.BlockSpec((None, 8, 1), lambda b, j: (j, 0, 0)),
            pl.BlockSpec((None, 1, ML_DH), lambda b, j: (j, 0, 0)),
            pl.BlockSpec((1, ML_DH), lambda b, j: (0, 0)),
        ],
        out_specs=pl.BlockSpec((None, seq, ML_DH), lambda b, j: (b, 0, j)),
        out_shape=jax.ShapeDtypeStruct((bsz, seq, D_INNER), BF16),
        scratch_shapes=[pltpu.VMEM((seq, ML_DK), BF16),
                        pltpu.VMEM((seq, ML_DK), BF16),
                        pltpu.VMEM((seq, ML_DH), BF16),
                        pltpu.VMEM((seq, ML_DH), BF16),
                        pltpu.VMEM((seq, ML_DH), F32),
                        pltpu.VMEM((2, ML_DK, ML_DH), F32),
                        pltpu.VMEM((2, seq, ML_DH), BF16),
                        pltpu.VMEM((2, 3, seq, LANES), F32),
                        pltpu.VMEM((nc, ML_DK, q), BF16),
                        pltpu.VMEM((2, nc, 8, q), F32),
                        pltpu.VMEM((2, ML_DK, ML_DH), BF16),
                        pltpu.VMEM((2, q + ML_DK, ML_DH), F32),
                        pltpu.VMEM((2, nc, 8, LANES), F32),
                        pltpu.VMEM((2, 2, q, LANES), F32),
                        pltpu.VMEM((2, 8, ML_DH), F32)],
        compiler_params=_params("parallel", "parallel"),
        name="mlstm_mixer",
    )(pm, pm, pm, g_col, g_row, conv_w.astype(F32), conv_b.astype(F32).reshape(1, -1),
      w_q.astype(BF16), w_k.astype(BF16), w_v.astype(BF16),
      gb.reshape(nh, 1, 4), jnp.pad(gb, ((0, 0), (0, 4))).reshape(nh, 8, 1),
      skip.astype(F32).reshape(nh, 1, ML_DH), onorm.astype(F32).reshape(1, ML_DH))
    out = _outproj(y.reshape(bsz * seq, D_INNER), w_out, h2, final_g)
    return out.reshape(bsz, seq, D_MODEL)


def kernel(x, ssd_norm, ssd_w_in, ssd_conv_w, ssd_conv_b, ssd_dt_bias, ssd_a_log, ssd_d, ssd_gnorm, ssd_w_out, gla_norm, gla_w_in, gla_w_gate, gla_b_gate, gla_onorm, gla_w_out, hy_norm, hy_w_in, hy_conv_w, hy_conv_b, hy_ffn_w_in, hy_ffn_b_in, hy_ffn_w_hid, hy_ffn_b_hid, hy_ffn_freq, hy_ffn_w_out, hy_d, hy_w_out, ml_norm, ml_w_in, ml_conv_w, ml_conv_b, ml_w_q, ml_w_k, ml_w_v, ml_gate_b, ml_skip, ml_onorm, ml_w_out, final_norm):
    depth = ssd_norm.shape[0] + gla_norm.shape[0] + hy_norm.shape[0] + ml_norm.shape[0]
    h = x
    for i in range(depth):
        kind, j = i % 4, i // 4
        fg = final_norm if i == depth - 1 else None
        if kind == 0:
            h = _ssd_layer(h, ssd_norm[j], ssd_w_in[j], ssd_conv_w[j], ssd_conv_b[j], ssd_dt_bias[j],
                           ssd_a_log[j], ssd_d[j], ssd_gnorm[j], ssd_w_out[j], fg)
        elif kind == 1:
            h = _gla_layer(h, gla_norm[j], gla_w_in[j], gla_w_gate[j], gla_b_gate[j], gla_onorm[j],
                           gla_w_out[j], fg)
        elif kind == 2:
            h = _hyena_layer(h, hy_norm[j], hy_w_in[j], hy_conv_w[j], hy_conv_b[j], hy_ffn_w_in[j],
                             hy_ffn_b_in[j], hy_ffn_w_hid[j], hy_ffn_b_hid[j], hy_ffn_freq[j],
                             hy_ffn_w_out[j], hy_d[j], hy_w_out[j], fg)
        else:
            h = _mlstm_layer(h, ml_norm[j], ml_w_in[j], ml_conv_w[j], ml_conv_b[j], ml_w_q[j], ml_w_k[j],
                             ml_w_v[j], ml_gate_b[j], ml_skip[j], ml_onorm[j], ml_w_out[j], fg)
    return h
```

```python
import functools
import math

import jax
import jax.numpy as jnp
from jax import lax
from jax.experimental import pallas as pl
from jax.experimental.pallas import tpu as pltpu

F32 = jnp.float32
BF16 = jnp.bfloat16

D_MODEL = 1024
D_INNER = 2 * D_MODEL
EPS = 1e-6

SSD_HEAD_DIM = 64
SSD_HEADS = D_INNER // SSD_HEAD_DIM
SSD_GROUPS = 8
SSD_HPG = SSD_HEADS // SSD_GROUPS
SSD_STATE = 128
SSD_CHUNK = 128
SSD_GW = SSD_HPG * SSD_HEAD_DIM
SSD_BC = SSD_GROUPS * SSD_STATE

GLA_HEADS = 4
GLA_DK = D_MODEL // 2 // GLA_HEADS
GLA_DV = D_INNER // GLA_HEADS
GLA_RANK = 16
GLA_NORMALIZER = 16.0
GLA_CHUNK = 64
GLA_QK = GLA_HEADS * GLA_DK

HY_ORDER = 2
HY_EMB = 33
HY_BANDS = (HY_EMB - 1) // 2
HY_FFN = 64
HY_INNER = 2
HY_FAST_PCT = 0.3
HY_SLOW_PCT = 1.5
HY_TARGET = 1e-2

ML_HEADS = 4
ML_DH = D_INNER // ML_HEADS
ML_DK = ML_DH // 2
ML_CHUNK = 128

LANES = 128
INPROJ_TM = 512
INPROJ_TN = 1024
VMEM_LIMIT = 56 * 1024 * 1024

_NEG_INF = float("-inf")


def _params(*sem):
    return pltpu.CompilerParams(dimension_semantics=sem, vmem_limit_bytes=VMEM_LIMIT)


def _sigmoid(x):
    return 0.5 * jnp.tanh(0.5 * x) + 0.5


def _silu(x):
    half = 0.5 * x
    return half + half * jnp.tanh(half)


def _softplus(x):
    return jnp.maximum(x, 0.0) + jnp.log1p(jnp.exp(-jnp.abs(x)))


def _log_sigmoid(x):
    return -_softplus(-x)


def _dot(a, b):
    return jnp.dot(a, b, preferred_element_type=F32)


def _dot_nt(a, b):
    return lax.dot_general(a, b, (((1,), (1,)), ((), ())), preferred_element_type=F32)


def _dot_tn(a, b):
    return lax.dot_general(a, b, (((0,), (0,)), ((), ())), preferred_element_type=F32)


def _dot_f32(a, b):
    return jnp.dot(a, b, preferred_element_type=F32, precision=lax.Precision.HIGHEST)


def _split(v):
    hi = v.astype(BF16)
    lo = (v - hi.astype(F32)).astype(BF16)
    return hi, lo


def _tri_dot(t, v):
    hi, lo = _split(v)
    return _dot(t, hi) + _dot(t, lo)


def _dot_tri(v, t):
    hi, lo = _split(v)
    return _dot(hi, t) + _dot(lo, t)


def _tri_consts(q):
    ri = lax.broadcasted_iota(jnp.int32, (q, q), 0)
    ci = lax.broadcasted_iota(jnp.int32, (q, q), 1)
    lower = ri >= ci
    upper = ci >= ri
    t_lower = jnp.where(lower, 1.0, 0.0).astype(BF16)
    t_upper = jnp.where(upper, 1.0, 0.0).astype(BF16)
    return (lower, upper), (t_lower, t_upper)


def _shift_rows(x, k):
    if k == 0:
        return x
    n = x.shape[0]
    rows = lax.broadcasted_iota(jnp.int32, x.shape, 0)
    y = pltpu.roll(x, k % n, axis=0)
    if k > 0:
        return jnp.where(rows >= k, y, 0.0)
    return jnp.where(rows < n + k, y, 0.0)


def _dwconv(x, w, b):
    taps = w.shape[0]
    pad = (taps - 1) // 2
    acc = x * w[pad:pad + 1, :] + b
    for j in range(taps):
        if j != pad:
            acc = acc + _shift_rows(x, pad - j) * w[j:j + 1, :]
    return acc


def _local_unroll(nc, limit=8):
    return next(u for u in (8, 4, 2, 1) if u <= limit and nc % u == 0)


CONV_HALO = 8


def _conv_blocks(src_ref, pad_ref, w, b, rb, emit):
    seq = src_ref.shape[0]
    taps = w.shape[0]
    pad = (taps - 1) // 2
    assert pad < CONV_HALO
    zeros = jnp.zeros((CONV_HALO, pad_ref.shape[1]), F32)
    pad_ref[0:CONV_HALO, :] = zeros
    pad_ref[CONV_HALO + seq:CONV_HALO + seq + CONV_HALO, :] = zeros
    pad_ref[CONV_HALO:CONV_HALO + seq, :] = src_ref[...].astype(F32)
    n = rb + 2 * CONV_HALO
    for r0 in range(0, seq, rb):
        win = pad_ref[r0:r0 + n, :]
        acc = b
        for j in range(taps):
            shift = pad - j
            moved = win if shift == 0 else pltpu.roll(win, shift % n, axis=0)
            acc = acc + moved[CONV_HALO:CONV_HALO + rb, :] * w[j:j + 1, :]
        emit(r0, acc)


def _rms(x):
    return x * lax.rsqrt(jnp.mean(x * x, axis=-1, keepdims=True) + EPS)


def _inproj_kernel(*refs, tn, with_gates):
    if with_gates:
        x_ref, g_ref, w_ref, wg_ref, o_ref, og_ref = refs
    else:
        x_ref, g_ref, w_ref, o_ref = refs
    xn = (_rms(x_ref[...]) * g_ref[...]).astype(BF16)
    for c0 in range(0, w_ref.shape[1], tn):
        o_ref[:, c0:c0 + tn] = _dot(xn, w_ref[:, c0:c0 + tn]).astype(o_ref.dtype)
    if with_gates:
        og_ref[...] = _dot(xn, wg_ref[...])


def _inproj(x2d, g, w, w_gate, tn):
    m, k = x2d.shape
    n = w.shape[1]
    tm = min(INPROJ_TM, m)
    with_gates = w_gate is not None
    in_specs = [pl.BlockSpec((tm, k), lambda i: (i, 0)),
                pl.BlockSpec((1, k), lambda i: (0, 0)),
                pl.BlockSpec((k, n), lambda i: (0, 0), pipeline_mode=pl.Buffered(1))]
    out_specs = [pl.BlockSpec((tm, n), lambda i: (i, 0))]
    out_shape = [jax.ShapeDtypeStruct((m, n), BF16)]
    args = [x2d, g.reshape(1, k).astype(F32), w]
    if with_gates:
        in_specs.append(pl.BlockSpec((k, LANES), lambda i: (0, 0), pipeline_mode=pl.Buffered(1)))
        out_specs.append(pl.BlockSpec((tm, LANES), lambda i: (i, 0)))
        out_shape.append(jax.ShapeDtypeStruct((m, LANES), F32))
        args.append(w_gate)
    outs = pl.pallas_call(
        functools.partial(_inproj_kernel, tn=tn, with_gates=with_gates),
        grid=(m // tm,),
        in_specs=in_specs,
        out_specs=out_specs,
        out_shape=out_shape,
        compiler_params=_params("parallel"),
        name="inproj",
    )(*args)
    return (outs[0], outs[1]) if with_gates else (outs[0], None)


def _outproj_kernel(y_ref, w_ref, r_ref, g_ref, o_ref, *, final):
    acc = _dot(y_ref[...], w_ref[...]) + r_ref[...]
    if final:
        acc = _rms(acc) * g_ref[...]
    o_ref[...] = acc


def _outproj(y2d, w, res2d, final_g=None):
    m, k = y2d.shape
    n = w.shape[1]
    tm = min(512, m)
    final = final_g is not None
    g = (final_g if final else jnp.ones((n,), F32)).reshape(1, n).astype(F32)
    return pl.pallas_call(
        functools.partial(_outproj_kernel, final=final),
        grid=(m // tm,),
        in_specs=[pl.BlockSpec((tm, k), lambda i: (i, 0)),
                  pl.BlockSpec((k, n), lambda i: (0, 0)),
                  pl.BlockSpec((tm, n), lambda i: (i, 0)),
                  pl.BlockSpec((1, n), lambda i: (0, 0))],
        out_specs=pl.BlockSpec((tm, n), lambda i: (i, 0)),
        out_shape=jax.ShapeDtypeStruct((m, n), F32),
        compiler_params=_params("parallel"),
        name="outproj",
    )(y2d, w.astype(BF16), res2d, g)


def _pad_cols(w, n):
    return jnp.pad(w, ((0, 0), (0, n - w.shape[1])))


def _ssd_kernel(z_ref, x_ref, b_ref, c_ref, dtc_ref, dtr_ref,
                cwx_ref, cwb_ref, cwc_ref, cbx_ref, cbb_ref, cbc_ref,
                biasr_ref, biasc_ref, alogr_ref, alogc_ref, dskip_ref, gnorm_ref,
                o_ref, xs_ref, cs_ref, bt_ref, y_ref, s_ref, u_ref, e1_ref, dec_ref, padx_ref, padn_ref):
    q = SSD_CHUNK
    seq = x_ref.shape[0]
    nc = seq // q
    hpg = SSD_HPG
    assert hpg * SSD_HEAD_DIM == 2 * LANES
    (lower, upper), (t_lower, t_upper) = _tri_consts(q)
    lane_head = lax.shift_right_logical(
        lax.broadcasted_iota(jnp.int32, (1, SSD_GW), 1), int(math.log2(SSD_HEAD_DIM)))
    first_head = lax.broadcasted_iota(jnp.int32, (1, LANES), 1) < SSD_HEAD_DIM
    fwd_lane = lax.broadcasted_iota(jnp.int32, (1, 2 * hpg), 1) < hpg
    fwd_row = lax.broadcasted_iota(jnp.int32, (2 * hpg, 1), 0) < hpg

    def emit_x(r0, blk):
        xs_ref[r0:r0 + q, :] = _silu(blk)

    def emit_c(r0, blk):
        cs_ref[r0:r0 + q, :] = _silu(blk).astype(BF16)

    def emit_b(r0, blk):
        bt_ref[r0 // q] = _silu(blk).T.astype(BF16)

    _conv_blocks(x_ref, padx_ref, cwx_ref[...], cbx_ref[...], q, emit_x)
    _conv_blocks(c_ref, padn_ref, cwc_ref[...], cbc_ref[...], q, emit_c)
    _conv_blocks(b_ref, padn_ref, cwb_ref[...], cbb_ref[...], q, emit_b)

    bias_row = biasr_ref[...]
    bias_col = biasc_ref[...]
    a_row = -jnp.exp(alogr_ref[...])
    a_col = -jnp.exp(alogc_ref[...])
    dskip = dskip_ref[...]

    def rep(cols, j):
        return jnp.broadcast_to(cols[:, j:j + 1], (cols.shape[0], LANES))

    sel_dt = jnp.where(
        lax.broadcasted_iota(jnp.int32, (2 * hpg, 2 * SSD_GW), 0)
        == lax.shift_right_logical(lax.broadcasted_iota(jnp.int32, (2 * hpg, 2 * SSD_GW), 1),
                                   int(math.log2(SSD_HEAD_DIM))), 1.0, 0.0).astype(BF16)

    unroll = _local_unroll(nc)

    def local_body(i, carry):
        chunks = [i * unroll + j for j in range(unroll)]
        rows = [pl.ds(pl.multiple_of(c * q, q), q) for c in chunks]
        us = range(unroll)
        dt_col = [_softplus(dtc_ref[r, :] + bias_row) for r in rows]
        dt_row = [_softplus(dtr_ref[c] + bias_col) for c in chunks]
        la_col = [dt * a_row for dt in dt_col]
        la_row = [dt * a_col for dt in dt_row]
        pre_col = [_tri_dot(t_lower, la) for la in la_col]
        pre_row = [_dot_tri(la, t_upper) for la in la_row]
        cbs = [_dot(cs_ref[rows[j], :], bt_ref[chunks[j]]) for j in us]
        dt_exp = [_dot(dt.astype(BF16), sel_dt) for dt in dt_col]
        ys, xqs, cum_exps = [], [], []
        for j in us:
            cum_col = jnp.where(fwd_lane, pre_col[j], pre_col[j][q - 1:q, :] - pre_col[j] + la_col[j])
            cum_row = jnp.where(fwd_row, pre_row[j], pre_row[j][:, q - 1:q] - pre_row[j] + la_row[j])
            reps = [rep(cum_col, r) for r in range(2 * hpg)]
            ms = []
            for r in range(hpg):
                rb = hpg + r
                mf = jnp.exp(jnp.where(lower, reps[r] - cum_row[r:r + 1, :], _NEG_INF)) * dt_row[j][r:r + 1, :]
                mb = jnp.exp(jnp.where(upper, reps[rb] - cum_row[rb:rb + 1, :], _NEG_INF)) * dt_row[j][rb:rb + 1, :]
                ms.append((cbs[j] * (mf + mb)).astype(BF16))
            xq = xs_ref[rows[j], :]
            xb = xq.astype(BF16)
            xbd = jnp.concatenate(
                [jnp.where(lane_head == r, xb, jnp.zeros_like(xb)) for r in range(hpg)], axis=0)
            ys.append(_dot(jnp.concatenate(ms, axis=1), xbd))
            xqs.append(xq)
            cum_exps.append([jnp.concatenate(
                [jnp.where(first_head, reps[lo], reps[lo + 1]),
                 jnp.where(first_head, reps[lo + 2], reps[lo + 3])], axis=1) for lo in (0, hpg)])
        for j in us:
            y_ref[rows[j], :] = ys[j] + xqs[j] * dskip
            for d in range(2):
                cum_exp = cum_exps[j][d]
                tot = cum_exp[q - 1:q, :] if d == 0 else cum_exp[0:1, :]
                w = dt_exp[j][:, d * SSD_GW:(d + 1) * SSD_GW] * jnp.exp(tot - cum_exp)
                u_ref[d, chunks[j]] = _dot(bt_ref[chunks[j]], (xqs[j] * w).astype(BF16))
                e1_ref[d, rows[j], :] = jnp.exp(cum_exp).astype(BF16)
                dec_ref[d, chunks[j]] = jnp.exp(tot)
        return carry

    lax.fori_loop(0, nc // unroll, local_body, 0)

    s_ref[...] = jnp.zeros_like(s_ref)

    def serial(i, carry):
        chunks = (i, nc - 1 - i)
        rows = [pl.ds(pl.multiple_of(c * q, q), q) for c in chunks]
        states = [s_ref[d] for d in range(2)]
        inter = [_dot(cs_ref[rows[d], :], states[d].astype(BF16)) for d in range(2)]
        for d in range(2):
            s_ref[d] = states[d] * dec_ref[d, chunks[d]] + u_ref[d, chunks[d]]
        for d in range(2):
            y_ref[rows[d], :] += inter[d] * e1_ref[d, rows[d], :].astype(F32)
        return carry

    lax.fori_loop(0, nc, serial, 0)

    y = y_ref[...] * _silu(z_ref[...].astype(F32))
    o_ref[...] = (_rms(y) * gnorm_ref[...]).astype(o_ref.dtype)


def _ssd_layer(h, norm, w_in, conv_w, conv_b, dt_bias, a_log, d_skip, gnorm, w_out, final_g):
    bsz, seq, _ = h.shape
    g, hpg, q = SSD_GROUPS, SSD_HPG, SSD_CHUNK
    nc = seq // q
    h2 = h.reshape(bsz * seq, D_MODEL)
    n_main = 2 * D_INNER + 2 * SSD_BC
    pm, gates = _inproj(h2, norm, w_in[:, :n_main].astype(BF16),
                        _pad_cols(w_in[:, n_main:], LANES).astype(BF16), INPROJ_TN)
    pm = pm.reshape(bsz, seq, n_main)
    dt = gates[:, :2 * SSD_HEADS].reshape(bsz, seq, 2, g, hpg)
    dt_col = dt.transpose(0, 3, 1, 2, 4).reshape(bsz, g, seq, 2 * hpg)
    dt_row = dt.reshape(bsz, nc, q, 2, g, hpg).transpose(0, 4, 1, 3, 5, 2).reshape(bsz, g, nc, 2 * hpg, q)

    def per_group(p):
        return p.astype(F32).reshape(2, g, hpg).transpose(1, 0, 2).reshape(g, 2 * hpg)

    bias, alog = per_group(dt_bias), per_group(a_log)
    dskip = jnp.repeat(d_skip.astype(F32).reshape(g, hpg), SSD_HEAD_DIM, axis=1).reshape(g, 1, SSD_GW)
    cw = conv_w.astype(F32)
    cb = conv_b.astype(F32).reshape(1, -1)
    taps = cw.shape[0]
    xo = D_INNER // SSD_GW
    bo = 2 * D_INNER // SSD_STATE
    co = bo + SSD_GROUPS
    cbo = D_INNER // SSD_STATE
    cco = cbo + SSD_GROUPS

    y = pl.pallas_call(
        _ssd_kernel,
        grid=(bsz, g),
        in_specs=[
            pl.BlockSpec((None, seq, SSD_GW), lambda b, j: (b, 0, j)),
            pl.BlockSpec((None, seq, SSD_GW), lambda b, j: (b, 0, xo + j)),
            pl.BlockSpec((None, seq, SSD_STATE), lambda b, j: (b, 0, bo + j)),
            pl.BlockSpec((None, seq, SSD_STATE), lambda b, j: (b, 0, co + j)),
            pl.BlockSpec((None, None, seq, 2 * hpg), lambda b, j: (b, j, 0, 0)),
            pl.BlockSpec((None, None, nc, 2 * hpg, q), lambda b, j: (b, j, 0, 0, 0)),
            pl.BlockSpec((taps, SSD_GW), lambda b, j: (0, j)),
            pl.BlockSpec((taps, SSD_STATE), lambda b, j: (0, cbo + j)),
            pl.BlockSpec((taps, SSD_STATE), lambda b, j: (0, cco + j)),
            pl.BlockSpec((1, SSD_GW), lambda b, j: (0, j)),
            pl.BlockSpec((1, SSD_STATE), lambda b, j: (0, cbo + j)),
            pl.BlockSpec((1, SSD_STATE), lambda b, j: (0, cco + j)),
            pl.BlockSpec((None, 1, 2 * hpg), lambda b, j: (j, 0, 0)),
            pl.BlockSpec((None, 2 * hpg, 1), lambda b, j: (j, 0, 0)),
            pl.BlockSpec((None, 1, 2 * hpg), lambda b, j: (j, 0, 0)),
            pl.BlockSpec((None, 2 * hpg, 1), lambda b, j: (j, 0, 0)),
            pl.BlockSpec((None, 1, SSD_GW), lambda b, j: (j, 0, 0)),
            pl.BlockSpec((None, 1, SSD_GW), lambda b, j: (j, 0, 0)),
        ],
        out_specs=pl.BlockSpec((None, seq, SSD_GW), lambda b, j: (b, 0, j)),
        out_shape=jax.ShapeDtypeStruct((bsz, seq, D_INNER), BF16),
        scratch_shapes=[pltpu.VMEM((seq, SSD_GW), F32),
                        pltpu.VMEM((seq, SSD_STATE), BF16),
                        pltpu.VMEM((nc, SSD_STATE, q), BF16),
                        pltpu.VMEM((seq, SSD_GW), F32),
                        pltpu.VMEM((2, SSD_STATE, SSD_GW), F32),
                        pltpu.VMEM((2, nc, SSD_STATE, SSD_GW), F32),
                        pltpu.VMEM((2, seq, SSD_GW), BF16),
                        pltpu.VMEM((2, nc, 1, SSD_GW), F32),
                        pltpu.VMEM((seq + 2 * CONV_HALO, SSD_GW), F32),
                        pltpu.VMEM((seq + 2 * CONV_HALO, SSD_STATE), F32)],
        compiler_params=_params("parallel", "parallel"),
        name="ssd_mixer",
    )(pm, pm, pm, pm, dt_col, dt_row, cw, cw, cw, cb, cb, cb,
      bias.reshape(g, 1, 2 * hpg), bias.reshape(g, 2 * hpg, 1),
      alog.reshape(g, 1, 2 * hpg), alog.reshape(g, 2 * hpg, 1),
      dskip, gnorm.astype(F32).reshape(g, 1, SSD_GW))
    out = _outproj(y.reshape(bsz * seq, D_INNER), w_out, h2, final_g)
    return out.reshape(bsz, seq, D_MODEL)


def _gla_kernel(q_ref, k_ref, v_ref, z_ref, gl_ref, wg_ref, bg_ref, onorm_ref,
                o_ref, lg_ref, acc_ref, s_ref, qg_ref, u_ref, dec_ref):
    q = GLA_CHUNK
    seq = q_ref.shape[0]
    nc = seq // q
    masks, tris = _tri_consts(q)
    gl = gl_ref[...]
    for d in range(2):
        lg_ref[d] = _log_sigmoid(_dot_tri(gl, wg_ref[d]) + bg_ref[d]) * (1.0 / GLA_NORMALIZER)

    unroll = _local_unroll(nc)

    def local_body(i, carry):
        chunks = [i * unroll + j for j in range(unroll)]
        rows = [pl.ds(pl.multiple_of(c * q, q), q) for c in chunks]
        pairs = [(j, d) for j in range(unroll) for d in range(2)]
        cums = {(j, d): _tri_dot(tris[d], lg_ref[d, rows[j], :]) for j, d in pairs}
        qgs, kgs, kds = {}, {}, {}
        for j, d in pairs:
            cum = cums[j, d]
            qc = q_ref[rows[j], :].astype(F32) * (GLA_DK ** -0.5)
            kc = k_ref[rows[j], :].astype(F32)
            tot = cum[q - 1:q, :] if d == 0 else cum[0:1, :]
            qgs[j, d] = (qc * jnp.exp(cum)).astype(BF16)
            kgs[j, d] = (kc * jnp.exp(-cum)).astype(BF16)
            kds[j, d] = (kc * jnp.exp(tot - cum)).astype(BF16)
            qg_ref[d, rows[j], :] = qgs[j, d]
            dec_ref[d, chunks[j]] = jnp.exp(tot)
        atts = {p: _dot_nt(qgs[p], kgs[p]) for p in pairs}
        for j, d in pairs:
            u_ref[d, chunks[j]] = _dot_tn(v_ref[rows[j], :], kds[j, d]).astype(BF16)
        atts = {(j, d): jnp.where(masks[d], atts[j, d], 0.0).astype(BF16) for j, d in pairs}
        outs = {(j, d): _dot(atts[j, d], v_ref[rows[j], :]) for j, d in pairs}
        for j in range(unroll):
            acc_ref[rows[j], :] = outs[j, 0] + outs[j, 1]
        return carry

    lax.fori_loop(0, nc // unroll, local_body, 0)

    s_ref[...] = jnp.zeros_like(s_ref)

    def serial(i, carry):
        chunks = (i, nc - 1 - i)
        rows = [pl.ds(pl.multiple_of(c * q, q), q) for c in chunks]
        states = [s_ref[d] for d in range(2)]
        inter = [_dot_nt(qg_ref[d, rows[d], :], states[d].astype(BF16)) for d in range(2)]
        for d in range(2):
            s_ref[d] = states[d] * dec_ref[d, chunks[d]] + u_ref[d, chunks[d]].astype(F32)
        for d in range(2):
            acc_ref[rows[d], :] += inter[d]
        return carry

    lax.fori_loop(0, nc, serial, 0)

    out = _rms(acc_ref[...]) * onorm_ref[...]
    o_ref[...] = (out * _silu(z_ref[...].astype(F32))).astype(o_ref.dtype)


def _gla_layer(h, norm, w_in, w_gate, b_gate, onorm, w_out, final_g):
    bsz, seq, _ = h.shape
    nh = GLA_HEADS
    h2 = h.reshape(bsz * seq, D_MODEL)
    n_main = 2 * GLA_QK + 2 * D_INNER
    pm, gates = _inproj(h2, norm, w_in[:, :n_main].astype(BF16),
                        _pad_cols(w_in[:, n_main:], LANES).astype(BF16), INPROJ_TN)
    pm = pm.reshape(bsz, seq, n_main)
    gates = gates.reshape(bsz, seq, LANES)
    wg = jnp.zeros((2, LANES, GLA_QK), F32)
    for d in range(2):
        wg = wg.at[d, d * GLA_RANK:(d + 1) * GLA_RANK, :].set(w_gate[d].astype(F32))
    vo = 2 * GLA_QK // GLA_DV
    zo = vo + nh
    y = pl.pallas_call(
        _gla_kernel,
        grid=(bsz, nh),
        in_specs=[
            pl.BlockSpec((None, seq, GLA_DK), lambda b, j: (b, 0, j)),
            pl.BlockSpec((None, seq, GLA_DK), lambda b, j: (b, 0, nh + j)),
            pl.BlockSpec((None, seq, GLA_DV), lambda b, j: (b, 0, vo + j)),
            pl.BlockSpec((None, seq, GLA_DV), lambda b, j: (b, 0, zo + j)),
            pl.BlockSpec((None, seq, LANES), lambda b, j: (b, 0, 0)),
            pl.BlockSpec((2, LANES, GLA_DK), lambda b, j: (0, 0, j)),
            pl.BlockSpec((2, 1, GLA_DK), lambda b, j: (0, 0, j)),
            pl.BlockSpec((1, GLA_DV), lambda b, j: (0, 0)),
        ],
        out_specs=pl.BlockSpec((None, seq, GLA_DV), lambda b, j: (b, 0, j)),
        out_shape=jax.ShapeDtypeStruct((bsz, seq, D_INNER), BF16),
        scratch_shapes=[pltpu.VMEM((2, seq, GLA_DK), F32),
                        pltpu.VMEM((seq, GLA_DV), F32),
                        pltpu.VMEM((2, GLA_DV, GLA_DK), F32),
                        pltpu.VMEM((2, seq, GLA_DK), BF16),
                        pltpu.VMEM((2, seq // GLA_CHUNK, GLA_DV, GLA_DK), BF16),
                        pltpu.VMEM((2, seq // GLA_CHUNK, 1, GLA_DK), F32)],
        compiler_params=_params("parallel", "parallel"),
        name="gla_mixer",
    )(pm, pm, pm, pm, gates, wg.astype(BF16), b_gate.astype(F32).reshape(2, 1, GLA_QK),
      onorm.astype(F32).reshape(1, GLA_DV))
    out = _outproj(y.reshape(bsz * seq, D_INNER), w_out, h2, final_g)
    return out.reshape(bsz, seq, D_MODEL)


def _hy_mlp_kernel(f_ref, w0_ref, b0_ref, w1_ref, b1_ref, w2_ref, b2_ref, fr_ref, o_ref):
    fr = fr_ref[...]
    hid = jnp.sin(fr[0:1, :] * (_dot_f32(f_ref[...], w0_ref[...]) + b0_ref[...]))
    hid = jnp.sin(fr[1:2, :] * (_dot_f32(hid, w1_ref[...]) + b1_ref[...]))
    hid = jnp.sin(fr[2:3, :] * (_dot_f32(hid, w2_ref[...]) + b2_ref[...]))
    o_ref[...] = hid


def _hy_spec_kernel(hid_ref, wf_ref, wb_ref, t_ref, dl_ref, cos_ref, sin_ref, cosr_ref, sinr_ref,
                    kre_ref, ks_ref, p_ref):
    lb = hid_ref.shape[0] // 2

    @pl.when(pl.program_id(2) == 0)
    def _():
        env = jnp.exp(-t_ref[...] * dl_ref[...])
        hf = _dot_f32(hid_ref[...], wf_ref[...]) * env
        hb = _dot_f32(hid_ref[...], wb_ref[...]) * env
        first = lax.broadcasted_iota(jnp.int32, (lb, hf.shape[1]), 0) == 0
        hf_lo0 = jnp.where(first, 0.0, hf[0:lb])
        hb_lo0 = jnp.where(first, 0.0, hb[0:lb])
        p_ref[0] = (hf[0:lb] + hb_lo0).astype(BF16)
        p_ref[1] = (hf[0:lb] - hb_lo0).astype(BF16)
        p_ref[2] = hf[lb:].astype(BF16)
        p_ref[3] = hf_lo0.astype(BF16)
        p_ref[4] = hb[lb:].astype(BF16)
        p_ref[5] = hb_lo0.astype(BF16)

    scale = 1.0 / lb
    cos, sin, cosr, sinr = cos_ref[...], sin_ref[...], cosr_ref[...], sinr_ref[...]
    dt = kre_ref.dtype
    kre_ref[0] = (_dot(cos, p_ref[0]) * scale).astype(dt)
    ks_ref[0] = (_dot(sin, p_ref[1]) * scale).astype(dt)
    kre_ref[1] = ((_dot(cos, p_ref[2]) + _dot(cosr, p_ref[3])) * scale).astype(dt)
    ks_ref[1] = ((_dot(sin, p_ref[2]) + _dot(sinr, p_ref[3])) * scale).astype(dt)
    kre_ref[2] = ((_dot(cos, p_ref[4]) + _dot(cosr, p_ref[5])) * scale).astype(dt)
    ks_ref[2] = ((_dot(sin, p_ref[4]) + _dot(sinr, p_ref[5])) * (-scale)).astype(dt)


def _hy_conv_kernel(v_ref, x1_ref, x2_ref, z_ref, cwv_ref, cw1_ref, cw2_ref,
                    cbv_ref, cb1_ref, cb2_ref, kre_ref, ks_ref, d_ref, w_ref, o_ref,
                    y_ref, yb_ref, zf_ref, g_ref, yf_ref, *, fb):
    seq, tc = v_ref.shape
    lb = seq // 2
    nblk = lb // fb
    sb = min(64, fb)
    y_ref[...] = _dwconv(v_ref[...].astype(F32), cwv_ref[...], cbv_ref[...])
    gates = ((x1_ref, cw1_ref, cb1_ref), (x2_ref, cw2_ref, cb2_ref))
    for o, (x_ref, cw_ref, cb_ref) in enumerate(gates):
        yb_ref[:, 0:tc] = y_ref[0:lb, :].astype(BF16)
        yb_ref[:, tc:2 * tc] = y_ref[lb:seq, :].astype(BF16)
        g_ref[...] = _dwconv(x_ref[...].astype(F32), cw_ref[...], cb_ref[...])

        yb = yb_ref[...]
        for k in range(nblk):
            f0 = k * fb
            yf_ref[k, 0] = _dot(w_ref[f0:f0 + fb, :], yb)
            yf_ref[k, 1] = _dot(w_ref[lb + f0:lb + f0 + fb, :], yb)
        for k in range(nblk):
            for s0 in range(0, fb, sb):
                f0 = k * fb + s0
                re = [yf_ref[k, 0, s0:s0 + sb, 0:tc], yf_ref[k, 0, s0:s0 + sb, tc:2 * tc]]
                im = [yf_ref[k, 1, s0:s0 + sb, 0:tc], yf_ref[k, 1, s0:s0 + sb, tc:2 * tc]]
                kre = [kre_ref[o, n, f0:f0 + sb, :].astype(F32) for n in range(3)]
                ks = [ks_ref[o, n, f0:f0 + sb, :].astype(F32) for n in range(3)]
                for i, (n0, n1) in enumerate(((0, 2), (1, 0))):
                    zre = re[0] * kre[n0] - im[0] * ks[n0] + re[1] * kre[n1] - im[1] * ks[n1]
                    zim = im[0] * kre[n0] + re[0] * ks[n0] + im[1] * kre[n1] + re[1] * ks[n1]
                    zf_ref[f0:f0 + sb, i * tc:(i + 1) * tc] = zre.astype(BF16)
                    zf_ref[lb + f0:lb + f0 + sb, i * tc:(i + 1) * tc] = zim.astype(BF16)

        for k in range(nblk):
            t0 = k * fb
            conv = (_dot(w_ref[t0:t0 + fb, :], zf_ref[0:lb, :])
                    + _dot(w_ref[lb + t0:lb + t0 + fb, :], zf_ref[lb:2 * lb, :]))
            for i in range(2):
                r0 = i * lb + t0
                y_ref[r0:r0 + fb, :] = g_ref[r0:r0 + fb, :] * (conv[:, i * tc:(i + 1) * tc]
                                                               + y_ref[r0:r0 + fb, :] * d_ref[o:o + 1, :])
    o_ref[...] = (y_ref[...] * _silu(z_ref[...].astype(F32))).astype(o_ref.dtype)


def _hyena_layer(h, norm, w_in, conv_w, conv_b, ffn_w_in, ffn_b_in, ffn_w_hid, ffn_b_hid,
                 ffn_freq, ffn_w_out, d_bias, w_out, final_g):
    bsz, seq, _ = h.shape
    c = D_INNER
    h2 = h.reshape(bsz * seq, D_MODEL)
    pm = _inproj(h2, norm, w_in.astype(BF16), None, INPROJ_TN)[0].reshape(bsz, seq, 4 * c)

    t = jnp.linspace(0.0, 1.0, seq, dtype=F32)[:, None]
    pos = jnp.arange(seq, dtype=F32)[:, None]
    bands = jnp.linspace(1e-4, HY_BANDS - 1, HY_BANDS, dtype=F32)[None]
    ang = (2.0 * math.pi / seq) * pos * bands
    feats = jnp.concatenate([t, jnp.cos(ang), -jnp.sin(ang)], axis=-1)
    feats = _pad_cols(feats, LANES)
    w0 = jnp.pad(ffn_w_in.astype(F32), ((0, LANES - HY_EMB), (0, 0)))
    hid = pl.pallas_call(
        _hy_mlp_kernel,
        out_shape=jax.ShapeDtypeStruct((seq, HY_FFN), F32),
        name="hyena_filter_mlp",
    )(feats, w0, ffn_b_in.astype(F32).reshape(1, HY_FFN),
      ffn_w_hid[0].astype(F32), ffn_b_hid[0].astype(F32).reshape(1, HY_FFN),
      ffn_w_hid[1].astype(F32), ffn_b_hid[1].astype(F32).reshape(1, HY_FFN),
      ffn_freq.astype(F32))

    max_decay = math.log(HY_TARGET) / HY_FAST_PCT
    min_decay = math.log(HY_TARGET) / HY_SLOW_PCT
    deltas = jnp.abs(jnp.linspace(min_decay, max_decay, c, dtype=F32)).reshape(1, c)

    lb = seq // 2
    fi = jnp.arange(lb, dtype=jnp.int32)
    n4 = 8 * lb

    def table(num, fn):
        return fn((2.0 * math.pi / n4) * (num % n4).astype(F32)).astype(BF16)

    num1 = (2 * fi[:, None] + 1) * (2 * fi[None, :])
    numr = (2 * fi[:, None] + 1) * (2 * (fi[None, :] - lb))
    num2 = (2 * fi[:, None] + 1) * (2 * fi[None, :] + 1)
    cos1, sin1 = table(num1, jnp.cos), table(num1, jnp.sin)
    cosr, sinr = table(numr, jnp.cos), table(numr, jnp.sin)
    w2 = jnp.concatenate([table(num2, jnp.cos), table(num2, jnp.sin)], axis=0)

    tc = min(512, c)
    fb = min(512, lb)
    nct = c // tc
    wo = ffn_w_out.astype(F32)
    tab_spec = pl.BlockSpec((fb, lb), lambda o, j, k: (k, 0))
    kre, ks = pl.pallas_call(
        _hy_spec_kernel,
        grid=(HY_ORDER, nct, lb // fb),
        in_specs=[
            pl.BlockSpec((seq, HY_FFN), lambda o, j, k: (0, 0)),
            pl.BlockSpec((HY_FFN, tc), lambda o, j, k: (0, o * nct + j)),
            pl.BlockSpec((HY_FFN, tc), lambda o, j, k: (0, (HY_ORDER + o) * nct + j)),
            pl.BlockSpec((seq, 1), lambda o, j, k: (0, 0)),
            pl.BlockSpec((1, tc), lambda o, j, k: (0, j)),
            tab_spec, tab_spec, tab_spec, tab_spec,
        ],
        out_specs=[pl.BlockSpec((None, 3, fb, tc), lambda o, j, k: (o, 0, k, j)),
                   pl.BlockSpec((None, 3, fb, tc), lambda o, j, k: (o, 0, k, j))],
        out_shape=[jax.ShapeDtypeStruct((HY_ORDER, 3, lb, c), BF16)] * 2,
        scratch_shapes=[pltpu.VMEM((6, lb, tc), BF16)],
        compiler_params=_params("parallel", "parallel", "arbitrary"),
        name="hyena_filter_spectrum",
    )(hid, wo, wo, t, deltas, cos1, sin1, cosr, sinr)

    tcc = min(128, c)
    ncc = c // tcc
    cw = conv_w.astype(F32)
    cb = conv_b.astype(F32).reshape(1, -1)
    taps = cw.shape[0]
    y = pl.pallas_call(
        functools.partial(_hy_conv_kernel, fb=fb),
        grid=(ncc, bsz),
        in_specs=[
            pl.BlockSpec((None, seq, tcc), lambda j, b: (b, 0, j)),
            pl.BlockSpec((None, seq, tcc), lambda j, b: (b, 0, ncc + j)),
            pl.BlockSpec((None, seq, tcc), lambda j, b: (b, 0, 2 * ncc + j)),
            pl.BlockSpec((None, seq, tcc), lambda j, b: (b, 0, 3 * ncc + j)),
            pl.BlockSpec((taps, tcc), lambda j, b: (0, j)),
            pl.BlockSpec((taps, tcc), lambda j, b: (0, ncc + j)),
            pl.BlockSpec((taps, tcc), lambda j, b: (0, 2 * ncc + j)),
            pl.BlockSpec((1, tcc), lambda j, b: (0, j)),
            pl.BlockSpec((1, tcc), lambda j, b: (0, ncc + j)),
            pl.BlockSpec((1, tcc), lambda j, b: (0, 2 * ncc + j)),
            pl.BlockSpec((HY_ORDER, 3, lb, tcc), lambda j, b: (0, 0, 0, j), pipeline_mode=pl.Buffered(1)),
            pl.BlockSpec((HY_ORDER, 3, lb, tcc), lambda j, b: (0, 0, 0, j), pipeline_mode=pl.Buffered(1)),
            pl.BlockSpec((HY_ORDER, tcc), lambda j, b: (0, j)),
            pl.BlockSpec((2 * lb, lb), lambda j, b: (0, 0), pipeline_mode=pl.Buffered(1)),
        ],
        out_specs=pl.BlockSpec((None, seq, tcc), lambda j, b: (b, 0, j)),
        out_shape=jax.ShapeDtypeStruct((bsz, seq, c), BF16),
        scratch_shapes=[pltpu.VMEM((seq, tcc), F32),
                        pltpu.VMEM((lb, 2 * tcc), BF16),
                        pltpu.VMEM((2 * lb, 2 * tcc), BF16),
                        pltpu.VMEM((seq, tcc), F32),
                        pltpu.VMEM((lb // fb, 2, fb, 2 * tcc), F32)],
        compiler_params=_params("parallel", "parallel"),
        name="hyena_mixer",
    )(pm, pm, pm, pm, cw, cw, cw, cb, cb, cb, kre, ks, d_bias.astype(F32), w2)
    out = _outproj(y.reshape(bsz * seq, c), w_out, h2, final_g)
    return out.reshape(bsz, seq, D_MODEL)


def _mlstm_kernel(xm_ref, z_ref, og_ref, gc_ref, gr_ref, cw_ref, cb_ref, wq_ref, wk_ref, wv_ref,
                  gbr_ref, gbc_ref, skip_ref, onorm_ref,
                  o_ref, q_ref, k_ref, v_ref, ch_ref, acc_ref, c_ref, nl_ref, strep_ref, kt_ref, str_ref, cmat_ref,
                  par_ref, sc_ref, rowc_ref, chunkc_ref):
    q = ML_CHUNK
    seq = xm_ref.shape[0]
    nc = seq // q
    masks, tris = _tri_consts(q)

    for j in range(ML_DH // LANES):
        sl = slice(j * LANES, (j + 1) * LANES)
        ch_ref[:, sl] = _silu(_dwconv(xm_ref[:, sl].astype(F32), cw_ref[:, sl], cb_ref[:, sl])).astype(BF16)
    ch = ch_ref[...]
    q_ref[...] = _dot(ch, wq_ref[...]).astype(BF16)
    kf = _dot(ch, wk_ref[...]) * (ML_DK ** -0.5)
    k_ref[...] = kf.astype(BF16)
    for c in range(nc):
        kt_ref[c] = kf[c * q:(c + 1) * q, :].T.astype(BF16)
    v_ref[...] =_dot(xm_ref[...], wv_ref[...]).astype(BF16)

    gb_row = gbr_ref[...]
    gb_col = gbc_ref[...]

    (t_lower, t_upper) = tris

    unroll = _local_unroll(nc, limit=4)

    def local_body(i, carry):
        chunks = [i * unroll + j for j in range(unroll)]
        rows = [pl.ds(pl.multiple_of(c * q, q), q) for c in chunks]
        pairs = [(j, d) for j in range(unroll) for d in range(2)]
        g_col = [gc_ref[r, :] + gb_row for r in rows]
        g_row = [gr_ref[c] + gb_col for c in chunks]
        lf_col = [_log_sigmoid(g) for g in g_col]
        lf_row = [_log_sigmoid(g) for g in g_row]
        tri_c, tri_r = (t_lower, t_upper), (t_upper, t_lower)
        b_cols = {(j, d): _tri_dot(tri_c[d], lf_col[j]) for j, d in pairs}
        b_rows = {(j, d): _dot_tri(lf_row[j], tri_r[d]) for j, d in pairs}
        qk = [_dot_nt(q_ref[r, :], k_ref[r, :]) for r in rows]
        ss = {}
        for j, d in pairs:
            i_col = g_col[j][:, 2 * d:2 * d + 1]
            i_row = g_row[j][2 * d:2 * d + 1, :]
            b_col = b_cols[j, d][:, 2 * d + 1:2 * d + 2]
            b_row = b_rows[j, d][2 * d + 1:2 * d + 2, :]
            logd = jnp.where(masks[d], b_col - b_row + i_row, _NEG_INF)
            m_loc = jnp.max(logd, axis=1, keepdims=True)
            s = qk[j] * jnp.exp(logd - m_loc)
            ss[j, d] = s.astype(BF16)
            den_loc = jnp.sum(s, axis=1, keepdims=True)
            tot = b_col[q - 1:q, :] if d == 0 else b_col[0:1, :]
            lw_col = tot - b_col + i_col
            lw_max = jnp.max(lw_col, axis=0, keepdims=True)
            for n, stat in enumerate((m_loc, den_loc, b_col)):
                strep_ref[d, n, rows[j], :] = jnp.broadcast_to(stat, (q, LANES))
            sc_ref[d, chunks[j], 0:1, :] = jnp.broadcast_to(tot, (1, LANES))
            sc_ref[d, chunks[j], 1:2, :] = jnp.broadcast_to(lw_max, (1, LANES))
            str_ref[d, chunks[j]] = jnp.broadcast_to(tot - b_row + i_row - lw_max, (8, q))
        for j, d in pairs:
            nl_ref[d, rows[j], :] = _dot(ss[j, d], v_ref[rows[j], :]).astype(BF16)
        return carry

    lax.fori_loop(0, nc // unroll, local_body, 0)

    c_ref[...] = jnp.zeros_like(c_ref)
    cmat_ref[...] = jnp.zeros_like(cmat_ref)
    acc_ref[...] = jnp.zeros_like(acc_ref)

    nblk = 4
    cb, ab = ML_DK // nblk, q // nblk

    def tile(x, n):
        return jnp.concatenate([x] * n, axis=1)

    def serial(i, carry):
        chunks = (i, nc - 1 - i)
        rows = [pl.ds(pl.multiple_of(c * q, q), q) for c in chunks]
        w_rows = [jnp.exp(str_ref[d, chunks[d]]).astype(BF16) for d in range(2)]
        for d in range(2):
            par_ref[d, q:q + ML_DK, :] = _dot(kt_ref[chunks[d]] * w_rows[d][0:1, :], v_ref[rows[d], :])
        for d in range(2):
            par_ref[d, 0:q, :] = _dot(q_ref[rows[d], :], cmat_ref[d])
        new = []
        for d in range(2):
            n_row, m = carry[d]
            ksum = _dot(w_rows[d], k_ref[rows[d], :])[0:1, :]
            qn = _dot_nt(q_ref[rows[d], :],
                         jnp.broadcast_to(n_row, (LANES, ML_DK)).astype(BF16))
            m_loc = strep_ref[d, 0, rows[d], :]
            den_loc = strep_ref[d, 1, rows[d], :]
            inter = strep_ref[d, 2, rows[d], :] + m
            m_row = jnp.maximum(m_loc, inter)
            a_loc = jnp.exp(m_loc - m_row)
            a_int = jnp.exp(inter - m_row)
            den = den_loc * a_loc + qn * a_int
            scale = 1.0 / jnp.maximum(jnp.abs(den), jnp.exp(-m_row))
            rowc_ref[d, 0] = a_loc * scale
            rowc_ref[d, 1] = a_int * scale
            tot = sc_ref[d, chunks[d], 0:1, :]
            lw_max = sc_ref[d, chunks[d], 1:2, :]
            m_new = jnp.maximum(tot + m, lw_max)
            dec = jnp.exp(tot + m - m_new)
            gain = jnp.exp(lw_max - m_new)
            chunkc_ref[d, 0:1, :] = tile(dec, ML_DH // LANES)
            chunkc_ref[d, 1:2, :] = tile(gain, ML_DH // LANES)
            new.append((n_row * tile(dec, ML_DK // LANES) + ksum * tile(gain, ML_DK // LANES), m_new))

        def update(b, inner):
            for d in range(2):
                r = pl.ds(pl.multiple_of(b * cb, cb), cb)
                cnew = (c_ref[d, r, :] * chunkc_ref[d, 0:1, :]
                        + par_ref[d, pl.ds(pl.multiple_of(q + b * cb, cb), cb), :] * chunkc_ref[d, 1:2, :])
                c_ref[d, r, :] = cnew
                cmat_ref[d, r, :] = cnew.astype(BF16)
                rl = pl.ds(pl.multiple_of(b * ab, ab), ab)
                ra = pl.ds(pl.multiple_of(chunks[d] * q + b * ab, ab), ab)
                acc_ref[ra, :] += (nl_ref[d, ra, :].astype(F32) * tile(rowc_ref[d, 0, rl, :], ML_DH // LANES)
                                   + par_ref[d, rl, :] * tile(rowc_ref[d, 1, rl, :], ML_DH // LANES))
            return inner

        lax.fori_loop(0, nblk, update, 0)
        return tuple(new)

    init = (jnp.zeros((1, ML_DK), F32), jnp.full((1, LANES), _NEG_INF, F32))
    lax.fori_loop(0, nc, serial, (init, init))

    hh = _sigmoid(og_ref[...].astype(F32)) * acc_ref[...]
    hh = _rms(hh) * onorm_ref[...] + skip_ref[...] * ch_ref[...].astype(F32)
    o_ref[...] = (hh * _silu(z_ref[...].astype(F32))).astype(o_ref.dtype)


def _mlstm_layer(h, norm, w_in, conv_w, conv_b, w_q, w_k, w_v, gate_b, skip, onorm, w_out, final_g):
    bsz, seq, _ = h.shape
    nh, q = ML_HEADS, ML_CHUNK
    nc = seq // q
    h2 = h.reshape(bsz * seq, D_MODEL)
    n_main = 3 * D_INNER
    pm, gates = _inproj(h2, norm, w_in[:, :n_main].astype(BF16),
                        _pad_cols(w_in[:, n_main:], LANES).astype(BF16), INPROJ_TN)
    pm = pm.reshape(bsz, seq, n_main)
    gt = gates[:, :4 * nh].reshape(bsz, seq, 2, 2, nh)
    g_col = gt.transpose(0, 4, 1, 2, 3).reshape(bsz, nh, seq, 4)
    g_row = gt.reshape(bsz, nc, q, 2, 2, nh).transpose(0, 5, 1, 3, 4, 2).reshape(bsz, nh, nc, 4, q)
    g_row = jnp.pad(g_row, ((0, 0), (0, 0), (0, 0), (0, 4), (0, 0)))
    gb =gate_b.astype(F32).transpose(2, 0, 1).reshape(nh, 4)
    taps = conv_w.shape[0]
    y = pl.pallas_call(
        _mlstm_kernel,
        grid=(bsz, nh),
        in_specs=[
            pl.BlockSpec((None, seq, ML_DH), lambda b, j: (b, 0, j)),
            pl.BlockSpec((None, seq, ML_DH), lambda b, j: (b, 0, nh + j)),
            pl.BlockSpec((None, seq, ML_DH), lambda b, j: (b, 0, 2 * nh + j)),
            pl.BlockSpec((None, None, seq, 4), lambda b, j: (b, j, 0, 0)),
            pl.BlockSpec((None, None, nc, 8, q), lambda b, j: (b, j, 0, 0, 0)),
            pl.BlockSpec((taps, ML_DH), lambda b, j: (0, j)),
            pl.BlockSpec((1, ML_DH), lambda b, j: (0, j)),
            pl.BlockSpec((None, ML_DH, ML_DK), lambda b, j: (j, 0, 0)),
            pl.BlockSpec((None, ML_DH, ML_DK), lambda b, j: (j, 0, 0)),
            pl.BlockSpec((None, ML_DH, ML_DH), lambda b, j: (j, 0, 0)),
            pl.BlockSpec((None, 1, 4), lambda b, j: (j, 0, 0)),
            pl.BlockSpec((None, 8, 1), lambda b, j: (j, 0, 0)),
            pl.BlockSpec((None, 1, ML_DH), lambda b, j: (j, 0, 0)),
            pl.BlockSpec((1, ML_DH), lambda b, j: (0, 0)),
        ],
        out_specs=pl.BlockSpec((None, seq, ML_DH), lambda b, j: (b, 0, j)),
        out_shape=jax.ShapeDtypeStruct((bsz, seq, D_INNER), BF16),
        scratch_shapes=[pltpu.VMEM((seq, ML_DK), BF16),
                        pltpu.VMEM((seq, ML_DK), BF16),
                        pltpu.VMEM((seq, ML_DH), BF16),
                        pltpu.VMEM((seq, ML_DH), BF16),
                        pltpu.VMEM((seq, ML_DH), F32),
                        pltpu.VMEM((2, ML_DK, ML_DH), F32),
                        pltpu.VMEM((2, seq, ML_DH), BF16),
                        pltpu.VMEM((2, 3, seq, LANES), F32),
                        pltpu.VMEM((nc, ML_DK, q), BF16),
                        pltpu.VMEM((2, nc, 8, q), F32),
                        pltpu.VMEM((2, ML_DK, ML_DH), BF16),
                        pltpu.VMEM((2, q + ML_DK, ML_DH), F32),
                        pltpu.VMEM((2, nc, 8, LANES), F32),
                        pltpu.VMEM((2, 2, q, LANES), F32),
                        pltpu.VMEM((2, 8, ML_DH), F32)],
        compiler_params=_params("parallel", "parallel"),
        name="mlstm_mixer",
    )(pm, pm, pm, g_col, g_row, conv_w.astype(F32), conv_b.astype(F32).reshape(1, -1),
      w_q.astype(BF16), w_k.astype(BF16), w_v.astype(BF16),
      gb.reshape(nh, 1, 4), jnp.pad(gb, ((0, 0), (0, 4))).reshape(nh, 8, 1),
      skip.astype(F32).reshape(nh, 1, ML_DH), onorm.astype(F32).reshape(1, ML_DH))
    out = _outproj(y.reshape(bsz * seq, D_INNER), w_out, h2, final_g)
    return out.reshape(bsz, seq, D_MODEL)


def kernel(x, ssd_norm, ssd_w_in, ssd_conv_w, ssd_conv_b, ssd_dt_bias, ssd_a_log, ssd_d, ssd_gnorm, ssd_w_out, gla_norm, gla_w_in, gla_w_gate, gla_b_gate, gla_onorm, gla_w_out, hy_norm, hy_w_in, hy_conv_w, hy_conv_b, hy_ffn_w_in, hy_ffn_b_in, hy_ffn_w_hid, hy_ffn_b_hid, hy_ffn_freq, hy_ffn_w_out, hy_d, hy_w_out, ml_norm, ml_w_in, ml_conv_w, ml_conv_b, ml_w_q, ml_w_k, ml_w_v, ml_gate_b, ml_skip, ml_onorm, ml_w_out, final_norm):
    depth = ssd_norm.shape[0] + gla_norm.shape[0] + hy_norm.shape[0] + ml_norm.shape[0]
    h = x
    for i in range(depth):
        kind, j = i % 4, i // 4
        fg = final_norm if i == depth - 1 else None
        if kind == 0:
            h = _ssd_layer(h, ssd_norm[j], ssd_w_in[j], ssd_conv_w[j], ssd_conv_b[j], ssd_dt_bias[j],
                           ssd_a_log[j], ssd_d[j], ssd_gnorm[j], ssd_w_out[j], fg)
        elif kind == 1:
            h = _gla_layer(h, gla_norm[j], gla_w_in[j], gla_w_gate[j], gla_b_gate[j], gla_onorm[j],
                           gla_w_out[j], fg)
        elif kind == 2:
            h = _hyena_layer(h, hy_norm[j], hy_w_in[j], hy_conv_w[j], hy_conv_b[j], hy_ffn_w_in[j],
                             hy_ffn_b_in[j], hy_ffn_w_hid[j], hy_ffn_b_hid[j], hy_ffn_freq[j],
                             hy_ffn_w_out[j], hy_d[j], hy_w_out[j], fg)
        else:
            h = _mlstm_layer(h, ml_norm[j], ml_w_in[j], ml_conv_w[j], ml_conv_b[j], ml_w_q[j], ml_w_k[j],
                             ml_w_v[j], ml_gate_b[j], ml_skip[j], ml_onorm[j], ml_w_out[j], fg)
    return h
```
